```python
import math
import jax
import jax.numpy as jnp
from jax import lax
import numpy as np

D_MODEL = 4096
BATCH = 4
SEQ = 2048
DEPTH = 4
DEC_BATCH = 128
DEC_SEQ = 4
PAST_LEN = 16384
PAGE_SIZE = 128

N_META = 16
N_EVEN = (DEPTH + 1) // 2
N_ODD = DEPTH // 2
D_FF = 2 * D_MODEL
EPS = 1e-6
W_A = D_MODEL // 2
H_A = 16
BW_A = W_A // H_A
CONV_W = 4
RG_C = 8.0
W_B = D_MODEL // 2
DK_B = 128
H_B = W_B // DK_B
DV_B = W_B // H_B
F_B = H_B * DK_B
HGRN_CHUNK = 32
H_C = 8
DV_C = D_MODEL // H_C
DK_C = DV_C // 2
W_CV = H_C * DV_C
W_CK = H_C * DK_C
MLSTM_CHUNK = 128
F_BIAS = 3.0

IN_EVEN = 2 * W_A + 2 * F_B + 2 * W_B
IN_ODD = 2 * W_CK + 2 * W_CV + 2 * H_C
EVEN_SPLITS = [W_A, 2 * W_A, 2 * W_A + F_B, 2 * W_A + 2 * F_B, 2 * W_A + 2 * F_B + W_B]
ODD_SPLITS = [W_CK, 2 * W_CK, 2 * W_CK + W_CV, 2 * W_CK + 2 * W_CV, 2 * W_CK + 2 * W_CV + H_C]

kernel_name = 'hybrid_rglru_hgrn2_mlstm_macaron_step'


def _rmsnorm(x, g):
    xf = x.astype(jnp.float32)
    y = xf * lax.rsqrt(jnp.mean(xf * xf, axis=-1, keepdims=True) + EPS)
    return (y * g.astype(jnp.float32)).astype(x.dtype)


def _swiglu(u, wg, wu, wd):
    return (jax.nn.silu(u @ wg) * (u @ wu)) @ wd


def _segments(step, init, seqs, seg_lens, chunk):
    outs, start, carry = [], 0, init
    for L in seg_lens:
        c = math.gcd(L, chunk)
        n = L // c
        xs = tuple(jnp.swapaxes(a[:, start:start + L].reshape(a.shape[0], n, c, *a.shape[2:]), 0, 1) for a in seqs)
        carry, ys = lax.scan(step, carry, xs)
        ys = jnp.swapaxes(ys, 0, 1)
        outs.append(ys.reshape(ys.shape[0], L, *ys.shape[3:]))
        start += L
    return jnp.concatenate(outs, axis=1), carry


def _causal_conv(x, buf, w, b):
    xp = jnp.concatenate([buf, x], axis=1)
    T = x.shape[1]
    y = b
    for j in range(CONV_W):
        y = y + w[j] * xp[:, j:j + T]
    return y, xp[:, -(CONV_W - 1):]


def _rglru(x, h0, w_r, b_r, w_i, b_i, lam, at_start):
    bsz, T, _ = x.shape
    xb = x.reshape(bsz, T, H_A, BW_A)
    r = jax.nn.sigmoid(jnp.einsum('bthi,hij->bthj', xb, w_r).reshape(bsz, T, W_A) + b_r)
    i = jax.nn.sigmoid(jnp.einsum('bthi,hij->bthj', xb, w_i).reshape(bsz, T, W_A) + b_i)
    log_a = -RG_C * r * jax.nn.softplus(-lam)
    a = jnp.exp(log_a)
    mult = jnp.sqrt(-jnp.expm1(2.0 * log_a))
    if at_start:
        mult = mult.at[:, 0].set(1.0)
    bx = mult * i * x
    bx = bx.at[:, 0].add(a[:, 0] * h0)

    def comb(lhs, rhs):
        return (lhs[0] * rhs[0], rhs[0] * lhs[1] + rhs[1])

    _, h = lax.associative_scan(comb, (a, bx), axis=1)
    return h, h[:, -1]


def _hgrn_step(S, xs):
    q, k, logf, v = xs
    c = q.shape[1]
    cum = jnp.cumsum(logf, axis=1)
    causal = jnp.tril(jnp.ones((c, c), bool))[None, :, :, None, None]
    diff = jnp.where(causal, cum[:, :, None] - cum[:, None, :], -jnp.inf)
    scores = jnp.einsum('bthk,btshk,bshk->bhts', q, jnp.exp(diff), k)
    o = jnp.einsum('bthk,bhkv->bthv', q * jnp.exp(cum), S) + jnp.einsum('bhts,bshv->bthv', scores, v)
    last = cum[:, -1]
    S_new = jnp.exp(last)[..., None] * S + jnp.einsum('bshk,bshv->bhkv', k * jnp.exp(last[:, None] - cum), v)
    return S_new, o


def _mlstm_step(carry, xs):
    C, n, m = carry
    q, k, v, ig, logf = xs
    c = q.shape[1]
    cum = jnp.cumsum(logf, axis=1)
    causal = jnp.tril(jnp.ones((c, c), bool))[None, :, :, None]
    logw = jnp.where(causal, cum[:, :, None] - cum[:, None, :] + ig[:, None], -jnp.inf)
    log_inter = cum + m[:, None]
    m_t = jnp.maximum(log_inter, jnp.max(logw, axis=2))
    w = jnp.exp(logw - m_t[:, :, None])
    g = jnp.exp(log_inter - m_t)
    p = jnp.einsum('bthk,bshk->btsh', q, k) * w
    num = g[..., None] * jnp.einsum('bthk,bhkv->bthv', q, C) + jnp.einsum('btsh,bshv->bthv', p, v)
    den = g * jnp.einsum('bthk,bhk->bth', q, n) + jnp.sum(p, axis=2)
    h = num / jnp.maximum(jnp.abs(den), jnp.exp(-m_t))[..., None]
    m_new = m_t[:, -1]
    ws = jnp.exp(cum[:, -1:] - cum + ig - m_new[:, None])
    decay = jnp.exp(cum[:, -1] + m - m_new)
    C_new = decay[..., None, None] * C + jnp.einsum('bsh,bshk,bshv->bhkv', ws, k, v)
    n_new = decay[..., None] * n + jnp.einsum('bsh,bshk->bhk', ws, k)
    return (C_new, n_new, m_new), h


def _mix_even(u, conv_buf, h0, S0, w_in, conv_w, conv_b, w_r, b_r, w_i, b_i, lam, lb, norm_b, w_out, seg_lens, at_start):
    bsz, T, _ = u.shape
    proj = (u @ w_in).astype(jnp.float32)
    xa, ga, qb, fb, ib, gb = jnp.split(proj, EVEN_SPLITS, axis=-1)
    xc, conv_new = _causal_conv(xa, conv_buf.astype(jnp.float32), conv_w, conv_b)
    ha, h_new = _rglru(xc, h0.astype(jnp.float32), w_r, b_r, w_i, b_i, lam, at_start)
    ya = ha * jax.nn.gelu(ga)
    hk = (bsz, T, H_B, DK_B)
    f = lb + (1.0 - lb) * jax.nn.sigmoid(fb)
    q = jax.nn.silu(qb).reshape(hk)
    k = ((1.0 - lb) * jax.nn.sigmoid(-fb)).reshape(hk)
    logf = jnp.log(f).reshape(hk)
    v = ib.reshape(bsz, T, H_B, DV_B)
    ob, S_new = _segments(_hgrn_step, S0.astype(jnp.float32), (q, k, logf, v), seg_lens, HGRN_CHUNK)
    yb = _rmsnorm(ob.reshape(bsz, T, W_B), norm_b) * jax.nn.silu(gb)
    y = jnp.concatenate([ya, yb], axis=-1).astype(u.dtype) @ w_out
    return y, conv_new, h_new, S_new


def _mix_odd(u, C0, n0, m0, w_in, b_gates, norm_c, w_out, seg_lens):
    bsz, T, _ = u.shape
    proj = (u @ w_in).astype(jnp.float32)
    q, k, v, o, ig, fg = jnp.split(proj, ODD_SPLITS, axis=-1)
    ig = ig + b_gates[:H_C]
    logf = jax.nn.log_sigmoid(fg + b_gates[H_C:])
    q = q.reshape(bsz, T, H_C, DK_C) * (DK_C ** -0.5)
    k = k.reshape(bsz, T, H_C, DK_C)
    v = v.reshape(bsz, T, H_C, DV_C)
    init = (C0.astype(jnp.float32), n0.astype(jnp.float32), m0.astype(jnp.float32))
    h, (C, n, m) = _segments(_mlstm_step, init, (q, k, v, ig, logf), seg_lens, MLSTM_CHUNK)
    hn = _rmsnorm(h, norm_c.reshape(H_C, DV_C)).reshape(bsz, T, W_CV)
    y = (jax.nn.sigmoid(o) * hn).astype(u.dtype) @ w_out
    return y, C, n, m


def _trunk(x, conv, hst, S, C, n, m, W, seg_lens, at_start):
    lb_p = jax.nn.softmax(W['hgrn_lb_raw'].astype(jnp.float32), axis=0)
    lb_all = jnp.cumsum(lb_p, axis=0) - lb_p[0]
    o_conv, o_h, o_S, o_C, o_n, o_m = [], [], [], [], [], []
    for l in range(DEPTH):
        x = x + 0.5 * _swiglu(_rmsnorm(x, W['ffn1_norm'][l]), W['ffn1_w_gate'][l], W['ffn1_w_up'][l], W['ffn1_w_down'][l])
        u = _rmsnorm(x, W['mix_norm'][l])
        if l % 2 == 0:
            e = l // 2
            y, cv, hh, ss = _mix_even(u, conv[e], hst[e], S[e], W['even_w_in'][e], W['rglru_conv_w'][e], W['rglru_conv_b'][e],
                                      W['rglru_w_r'][e], W['rglru_b_r'][e], W['rglru_w_i'][e], W['rglru_b_i'][e],
                                      W['rglru_lambda'][e], lb_all[e], W['hgrn_norm'][e], W['even_w_out'][e], seg_lens, at_start)
            o_conv.append(cv)
            o_h.append(hh)
            o_S.append(ss)
        else:
            d = l // 2
            y, cc, nn_, mm = _mix_odd(u, C[d], n[d], m[d], W['odd_w_in'][d], W['mlstm_b_gates'][d], W['mlstm_norm'][d],
                                      W['odd_w_out'][d], seg_lens)
            o_C.append(cc)
            o_n.append(nn_)
            o_m.append(mm)
        x = x + y
        x = x + 0.5 * _swiglu(_rmsnorm(x, W['ffn2_norm'][l]), W['ffn2_w_gate'][l], W['ffn2_w_up'][l], W['ffn2_w_down'][l])
    x = _rmsnorm(x, W['final_norm'])
    return (x, jnp.stack(o_conv), jnp.stack(o_h), jnp.stack(o_S), jnp.stack(o_C), jnp.stack(o_n), jnp.stack(o_m))


def setup_inputs(seed: int = 0) -> dict:
    key = jax.random.key(seed)
    ks = iter(jax.random.split(key, 48))

    def nrm(shape, scale):
        return jax.random.normal(next(ks), shape, jnp.float32) * scale

    def gain(shape):
        return 1.0 + nrm(shape, 0.02)

    u = jax.random.uniform(next(ks), (N_EVEN, W_A), jnp.float32, 0.9, 0.999)
    s = u ** (1.0 / RG_C)
    lam = jnp.log(s) - jnp.log1p(-s)
    gate_offset = jnp.concatenate([jnp.zeros((H_C,), jnp.float32), jnp.full((H_C,), F_BIAS, jnp.float32)])
    return {
        'x_prompt': nrm((BATCH, SEQ, D_MODEL), 1.0),
        'x_sample': nrm((DEC_BATCH, DEC_SEQ, D_MODEL), 1.0),
        'state_rglru_conv': nrm((N_EVEN, DEC_BATCH, CONV_W - 1, W_A), 1.0),
        'state_rglru_h': nrm((N_EVEN, DEC_BATCH, W_A), 0.5),
        'state_hgrn_S': nrm((N_EVEN, DEC_BATCH, H_B, DK_B, DV_B), 0.5),
        'state_mlstm_C': nrm((N_ODD, DEC_BATCH, H_C, DK_C, DV_C), 0.5),
        'state_mlstm_n': nrm((N_ODD, DEC_BATCH, H_C, DK_C), 0.5),
        'state_mlstm_m': nrm((N_ODD, DEC_BATCH, H_C), 1.0),
        'meta_tokens': nrm((N_META, D_MODEL), 1.0),
        'ffn1_norm': gain((DEPTH, D_MODEL)),
        'ffn1_w_gate': nrm((DEPTH, D_MODEL, D_FF), D_MODEL ** -0.5),
        'ffn1_w_up': nrm((DEPTH, D_MODEL, D_FF), D_MODEL ** -0.5),
        'ffn1_w_down': nrm((DEPTH, D_FF, D_MODEL), D_FF ** -0.5),
        'mix_norm': gain((DEPTH, D_MODEL)),
        'ffn2_norm': gain((DEPTH, D_MODEL)),
        'ffn2_w_gate': nrm((DEPTH, D_MODEL, D_FF), D_MODEL ** -0.5),
        'ffn2_w_up': nrm((DEPTH, D_MODEL, D_FF), D_MODEL ** -0.5),
        'ffn2_w_down': nrm((DEPTH, D_FF, D_MODEL), D_FF ** -0.5),
        'even_w_in': nrm((N_EVEN, D_MODEL, IN_EVEN), D_MODEL ** -0.5),
        'rglru_conv_w': nrm((N_EVEN, CONV_W, W_A), CONV_W ** -0.5),
        'rglru_conv_b': nrm((N_EVEN, W_A), 0.02),
        'rglru_w_r': nrm((N_EVEN, H_A, BW_A, BW_A), BW_A ** -0.5),
        'rglru_b_r': nrm((N_EVEN, W_A), 0.1),
        'rglru_w_i': nrm((N_EVEN, H_A, BW_A, BW_A), BW_A ** -0.5),
        'rglru_b_i': nrm((N_EVEN, W_A), 0.1),
        'rglru_lambda': lam,
        'hgrn_lb_raw': nrm((N_EVEN, F_B), 1.0),
        'hgrn_norm': gain((N_EVEN, W_B)),
        'even_w_out': nrm((N_EVEN, W_A + W_B, D_MODEL), (W_A + W_B) ** -0.5),
        'odd_w_in': nrm((N_ODD, D_MODEL, IN_ODD), D_MODEL ** -0.5),
        'mlstm_b_gates': nrm((N_ODD, 2 * H_C), 0.1) + gate_offset[None],
        'mlstm_norm': gain((N_ODD, W_CV)),
        'odd_w_out': nrm((N_ODD, W_CV, D_MODEL), W_CV ** -0.5),
        'final_norm': gain((D_MODEL,)),
    }


def reference(x_prompt, x_sample, state_rglru_conv, state_rglru_h, state_hgrn_S, state_mlstm_C, state_mlstm_n,
              state_mlstm_m, meta_tokens, ffn1_norm, ffn1_w_gate, ffn1_w_up, ffn1_w_down, mix_norm, ffn2_norm,
              ffn2_w_gate, ffn2_w_up, ffn2_w_down, even_w_in, rglru_conv_w, rglru_conv_b, rglru_w_r, rglru_b_r,
              rglru_w_i, rglru_b_i, rglru_lambda, hgrn_lb_raw, hgrn_norm, even_w_out, odd_w_in, mlstm_b_gates,
              mlstm_norm, odd_w_out, final_norm):
    W = {
        'ffn1_norm': ffn1_norm, 'ffn1_w_gate': ffn1_w_gate, 'ffn1_w_up': ffn1_w_up, 'ffn1_w_down': ffn1_w_down,
        'mix_norm': mix_norm, 'ffn2_norm': ffn2_norm, 'ffn2_w_gate': ffn2_w_gate, 'ffn2_w_up': ffn2_w_up,
        'ffn2_w_down': ffn2_w_down, 'even_w_in': even_w_in, 'rglru_conv_w': rglru_conv_w, 'rglru_conv_b': rglru_conv_b,
        'rglru_w_r': rglru_w_r, 'rglru_b_r': rglru_b_r, 'rglru_w_i': rglru_w_i, 'rglru_b_i': rglru_b_i,
        'rglru_lambda': rglru_lambda, 'hgrn_lb_raw': hgrn_lb_raw, 'hgrn_norm': hgrn_norm, 'even_w_out': even_w_out,
        'odd_w_in': odd_w_in, 'mlstm_b_gates': mlstm_b_gates, 'mlstm_norm': mlstm_norm, 'odd_w_out': odd_w_out,
        'final_norm': final_norm,
    }
    bp = x_prompt.shape[0]
    meta = jnp.broadcast_to(meta_tokens.astype(x_prompt.dtype)[None], (bp, N_META, D_MODEL))
    xp = jnp.concatenate([meta, x_prompt], axis=1)
    f32 = jnp.float32
    yp, p_conv, p_h, p_S, p_C, p_n, p_m = _trunk(
        xp,
        jnp.zeros((N_EVEN, bp, CONV_W - 1, W_A), f32),
        jnp.zeros((N_EVEN, bp, W_A), f32),
        jnp.zeros((N_EVEN, bp, H_B, DK_B, DV_B), f32),
        jnp.zeros((N_ODD, bp, H_C, DK_C, DV_C), f32),
        jnp.zeros((N_ODD, bp, H_C, DK_C), f32),
        jnp.zeros((N_ODD, bp, H_C), f32),
        W, (N_META, x_prompt.shape[1]), True)
    ys, s_conv, s_h, s_S, s_C, s_n, s_m = _trunk(
        x_sample, state_rglru_conv, state_rglru_h, state_hgrn_S, state_mlstm_C, state_mlstm_n, state_mlstm_m,
        W, (x_sample.shape[1],), False)
    return (yp[:, N_META:], ys, p_conv, p_h, p_S, p_C, p_n, p_m, s_conv, s_h, s_S, s_C, s_n, s_m)
```

```python
import functools
import math

import jax
import jax.numpy as jnp
from jax import lax
from jax.experimental import pallas as pl
from jax.experimental.pallas import tpu as pltpu

F32 = jnp.float32
BF16 = jnp.bfloat16

D_MODEL = 4096
BATCH = 4
SEQ = 2048
DEPTH = 4
DEC_BATCH = 128
DEC_SEQ = 4
N_META = 16
D_FF = 2 * D_MODEL
EPS = 1e-6
W_A = D_MODEL // 2
H_A = 16
BW_A = W_A // H_A
CONV_W = 4
RG_C = 8.0
W_B = D_MODEL // 2
DK_B = 128
H_B = W_B // DK_B
DV_B = W_B // H_B
F_B = H_B * DK_B
HGRN_CHUNK = 32
H_C = 8
DV_C = D_MODEL // H_C
DK_C = DV_C // 2
W_CV = H_C * DV_C
W_CK = H_C * DK_C
MLSTM_CHUNK = 128
IN_EVEN = 2 * W_A + 2 * F_B + 2 * W_B
IN_ODD_MAIN = 2 * W_CK + 2 * W_CV
EVEN_SPLITS = [W_A, 2 * W_A, 2 * W_A + F_B, 2 * W_A + 2 * F_B, 2 * W_A + 2 * F_B + W_B]
ODD_SPLITS = [W_CK, 2 * W_CK, 2 * W_CK + W_CV, 2 * W_CK + 2 * W_CV, 2 * W_CK + 2 * W_CV + H_C]

ROWS_P = BATCH * SEQ
ROWS_S = DEC_BATCH * DEC_SEQ
ROWS_META = BATCH * N_META
ROW_S0 = ROWS_P
ROW_META0 = ROWS_P + ROWS_S
ROWS_USED = ROW_META0 + ROWS_META
ROWS = 8832
ROWS_PAD = ROWS - ROWS_USED

LANES = 128
VMEM_LIMIT = 56 * 1024 * 1024

TM = 1472
TM_DOWN = 736
TN = 256
TR_NORM = 384


def _cparams(*sem):
    return pltpu.CompilerParams(dimension_semantics=sem, vmem_limit_bytes=VMEM_LIMIT)


def _rmsnorm_kernel(x_ref, g_ref, o_ref):
    x = x_ref[...]
    y = x * lax.rsqrt(jnp.mean(x * x, axis=-1, keepdims=True) + EPS)
    o_ref[...] = (y * g_ref[...]).astype(o_ref.dtype)


def _rmsnorm(x, g, out_dtype):
    rows, d = x.shape
    return pl.pallas_call(
        _rmsnorm_kernel,
        grid=(rows // TR_NORM,),
        in_specs=[pl.BlockSpec((TR_NORM, d), lambda i: (i, 0)),
                  pl.BlockSpec((1, d), lambda i: (0, 0))],
        out_specs=pl.BlockSpec((TR_NORM, d), lambda i: (i, 0)),
        out_shape=jax.ShapeDtypeStruct((rows, d), out_dtype),
        compiler_params=_cparams("parallel"),
        name="rmsnorm",
    )(x, g.reshape(1, d))


def _mm_kernel(a_ref, w_ref, o_ref):
    o_ref[...] = jnp.dot(a_ref[...], w_ref[...].astype(BF16), preferred_element_type=F32)


def _matmul(a, w3, layer, n_cols, tn=TN):
    rows, k = a.shape
    return pl.pallas_call(
        _mm_kernel,
        grid=(rows // TM, n_cols // tn),
        in_specs=[pl.BlockSpec((TM, k), lambda m, n: (m, 0)),
                  pl.BlockSpec((None, k, tn), lambda m, n: (layer, 0, n))],
        out_specs=pl.BlockSpec((TM, tn), lambda m, n: (m, n)),
        out_shape=jax.ShapeDtypeStruct((rows, n_cols), F32),
        compiler_params=_cparams("parallel", "arbitrary"),
        name="matmul",
    )(a, w3)


def _gateup_kernel(a_ref, wg_ref, wu_ref, o_ref):
    a = a_ref[...]
    g = jnp.dot(a, wg_ref[...].astype(BF16), preferred_element_type=F32)
    u = jnp.dot(a, wu_ref[...].astype(BF16), preferred_element_type=F32)
    o_ref[...] = (jax.nn.silu(g) * u).astype(o_ref.dtype)


def _gateup(a, wg, wu, layer):
    rows, k = a.shape
    n = wg.shape[-1]
    wspec = pl.BlockSpec((None, k, TN), lambda m, j: (layer, 0, j))
    return pl.pallas_call(
        _gateup_kernel,
        grid=(rows // TM, n // TN),
        in_specs=[pl.BlockSpec((TM, k), lambda m, j: (m, 0)), wspec, wspec],
        out_specs=pl.BlockSpec((TM, TN), lambda m, j: (m, j)),
        out_shape=jax.ShapeDtypeStruct((rows, n), BF16),
        compiler_params=_cparams("parallel", "arbitrary"),
        name="gateup",
    )(a, wg, wu)


def _mm_res_kernel(a_ref, w_ref, x_ref, o_ref, *, scale):
    acc = jnp.dot(a_ref[...], w_ref[...].astype(BF16), preferred_element_type=F32)
    o_ref[...] = x_ref[...] + scale * acc


def _matmul_residual(a, w3, layer, x, scale, tm):
    rows, k = a.shape
    n = w3.shape[-1]
    return pl.pallas_call(
        functools.partial(_mm_res_kernel, scale=scale),
        grid=(rows // tm, n // TN),
        in_specs=[pl.BlockSpec((tm, k), lambda m, j: (m, 0)),
                  pl.BlockSpec((None, k, TN), lambda m, j: (layer, 0, j)),
                  pl.BlockSpec((tm, TN), lambda m, j: (m, j))],
        out_specs=pl.BlockSpec((tm, TN), lambda m, j: (m, j)),
        out_shape=jax.ShapeDtypeStruct((rows, n), F32),
        input_output_aliases={2: 0},
        compiler_params=_cparams("parallel", "arbitrary"),
        name="matmul_residual",
    )(a, w3, x)


def _ffn_half(x, norm_g, wg, wu, wd, layer):
    xn = _rmsnorm(x, norm_g[layer], BF16)
    h = _gateup(xn, wg, wu, layer)
    return _matmul_residual(h, wd, layer, x, 0.5, TM_DOWN)


def _jrmsnorm(x, g):
    y = x * lax.rsqrt(jnp.mean(x * x, axis=-1, keepdims=True) + EPS)
    return y * g


def _segments(step, init, seqs, seg_lens, chunk):
    outs, start, carry = [], 0, init
    for L in seg_lens:
        c = math.gcd(L, chunk)
        n = L // c
        xs = tuple(jnp.swapaxes(a[:, start:start + L].reshape(a.shape[0], n, c, *a.shape[2:]), 0, 1) for a in seqs)
        carry, ys = lax.scan(step, carry, xs)
        ys = jnp.swapaxes(ys, 0, 1)
        outs.append(ys.reshape(ys.shape[0], L, *ys.shape[3:]))
        start += L
    return jnp.concatenate(outs, axis=1), carry


def _causal_conv(x, buf, w, b):
    xp = jnp.concatenate([buf, x], axis=1)
    T = x.shape[1]
    y = b
    for j in range(CONV_W):
        y = y + w[j] * xp[:, j:j + T]
    return y, xp[:, -(CONV_W - 1):]


def _rglru(x, h0, w_r, b_r, w_i, b_i, lam, at_start):
    bsz, T, _ = x.shape
    xb = x.reshape(bsz, T, H_A, BW_A)
    r = jax.nn.sigmoid(jnp.einsum('bthi,hij->bthj', xb, w_r).reshape(bsz, T, W_A) + b_r)
    i = jax.nn.sigmoid(jnp.einsum('bthi,hij->bthj', xb, w_i).reshape(bsz, T, W_A) + b_i)
    log_a = -RG_C * r * jax.nn.softplus(-lam)
    a = jnp.exp(log_a)
    mult = jnp.sqrt(-jnp.expm1(2.0 * log_a))
    if at_start:
        mult = mult.at[:, 0].set(1.0)
    bx = mult * i * x
    bx = bx.at[:, 0].add(a[:, 0] * h0)

    def comb(lhs, rhs):
        return (lhs[0] * rhs[0], rhs[0] * lhs[1] + rhs[1])

    _, h = lax.associative_scan(comb, (a, bx), axis=1)
    return h, h[:, -1]


def _hgrn_step(S, xs):
    q, k, logf, v = xs
    c = q.shape[1]
    cum = jnp.cumsum(logf, axis=1)
    causal = jnp.tril(jnp.ones((c, c), bool))[None, :, :, None, None]
    diff = jnp.where(causal, cum[:, :, None] - cum[:, None, :], -jnp.inf)
    scores = jnp.einsum('bthk,btshk,bshk->bhts', q, jnp.exp(diff), k)
    o = jnp.einsum('bthk,bhkv->bthv', q * jnp.exp(cum), S) + jnp.einsum('bhts,bshv->bthv', scores, v)
    last = cum[:, -1]
    S_new = jnp.exp(last)[..., None] * S + jnp.einsum('bshk,bshv->bhkv', k * jnp.exp(last[:, None] - cum), v)
    return S_new, o


def _mlstm_step(carry, xs):
    C, n, m = carry
    q, k, v, ig, logf = xs
    c = q.shape[1]
    cum = jnp.cumsum(logf, axis=1)
    causal = jnp.tril(jnp.ones((c, c), bool))[None, :, :, None]
    logw = jnp.where(causal, cum[:, :, None] - cum[:, None, :] + ig[:, None], -jnp.inf)
    log_inter = cum + m[:, None]
    m_t = jnp.maximum(log_inter, jnp.max(logw, axis=2))
    w = jnp.exp(logw - m_t[:, :, None])
    g = jnp.exp(log_inter - m_t)
    p = jnp.einsum('bthk,bshk->btsh', q, k) * w
    num = g[..., None] * jnp.einsum('bthk,bhkv->bthv', q, C) + jnp.einsum('btsh,bshv->bthv', p, v)
    den = g * jnp.einsum('bthk,bhk->bth', q, n) + jnp.sum(p, axis=2)
    h = num / jnp.maximum(jnp.abs(den), jnp.exp(-m_t))[..., None]
    m_new = m_t[:, -1]
    ws = jnp.exp(cum[:, -1:] - cum + ig - m_new[:, None])
    decay = jnp.exp(cum[:, -1] + m - m_new)
    C_new = decay[..., None, None] * C + jnp.einsum('bsh,bshk,bshv->bhkv', ws, k, v)
    n_new = decay[..., None] * n + jnp.einsum('bsh,bshk->bhk', ws, k)
    return (C_new, n_new, m_new), h


def _jmix_even(proj, conv_buf, h0, S0, conv_w, conv_b, w_r, b_r, w_i, b_i, lam, lb, norm_b, seg_lens, at_start):
    bsz, T, _ = proj.shape
    xa, ga, qb, fb, ib, gb = jnp.split(proj, EVEN_SPLITS, axis=-1)
    xc, conv_new = _causal_conv(xa, conv_buf, conv_w, conv_b)
    ha, h_new = _rglru(xc, h0, w_r, b_r, w_i, b_i, lam, at_start)
    ya = ha * jax.nn.gelu(ga)
    hk = (bsz, T, H_B, DK_B)
    f = lb + (1.0 - lb) * jax.nn.sigmoid(fb)
    q = jax.nn.silu(qb).reshape(hk)
    k = ((1.0 - lb) * jax.nn.sigmoid(-fb)).reshape(hk)
    logf = jnp.log(f).reshape(hk)
    v = ib.reshape(bsz, T, H_B, DV_B)
    ob, S_new = _segments(_hgrn_step, S0, (q, k, logf, v), seg_lens, HGRN_CHUNK)
    yb = _jrmsnorm(ob.reshape(bsz, T, W_B), norm_b) * jax.nn.silu(gb)
    y = jnp.concatenate([ya, yb], axis=-1).astype(BF16)
    return y, conv_new, h_new, S_new


def _jmix_odd(proj, gates, C0, n0, m0, b_gates, norm_c, seg_lens):
    bsz, T, _ = proj.shape
    q, k, v, o = jnp.split(proj, ODD_SPLITS[:3], axis=-1)
    ig = gates[..., :H_C] + b_gates[:H_C]
    logf = jax.nn.log_sigmoid(gates[..., H_C:2 * H_C] + b_gates[H_C:])
    q = q.reshape(bsz, T, H_C, DK_C) * (DK_C ** -0.5)
    k = k.reshape(bsz, T, H_C, DK_C)
    v = v.reshape(bsz, T, H_C, DV_C)
    h, (C, n, m) = _segments(_mlstm_step, (C0, n0, m0), (q, k, v, ig, logf), seg_lens, MLSTM_CHUNK)
    hn = _jrmsnorm(h, norm_c.reshape(H_C, DV_C)).reshape(bsz, T, W_CV)
    y = (jax.nn.sigmoid(o) * hn).astype(BF16)
    return y, C, n, m


def _split_rows(a):
    c = a.shape[-1]
    main = a[:ROWS_P].reshape(BATCH, SEQ, c)
    meta = a[ROW_META0:ROWS_USED].reshape(BATCH, N_META, c)
    return jnp.concatenate([meta, main], axis=1), a[ROW_S0:ROW_META0].reshape(DEC_BATCH, DEC_SEQ, c)


def _join_rows(yp, ys):
    c = yp.shape[-1]
    return jnp.concatenate([yp[:, N_META:].reshape(ROWS_P, c), ys.reshape(ROWS_S, c),
                            yp[:, :N_META].reshape(ROWS_META, c), jnp.zeros((ROWS_PAD, c), yp.dtype)], axis=0)


def kernel(x_prompt, x_sample, state_rglru_conv, state_rglru_h, state_hgrn_S, state_mlstm_C, state_mlstm_n, state_mlstm_m, meta_tokens, ffn1_norm, ffn1_w_gate, ffn1_w_up, ffn1_w_down, mix_norm, ffn2_norm, ffn2_w_gate, ffn2_w_up, ffn2_w_down, even_w_in, rglru_conv_w, rglru_conv_b, rglru_w_r, rglru_b_r, rglru_w_i, rglru_b_i, rglru_lambda, hgrn_lb_raw, hgrn_norm, even_w_out, odd_w_in, mlstm_b_gates, mlstm_norm, odd_w_out, final_norm):
    meta = jnp.broadcast_to(meta_tokens[None], (BATCH, N_META, D_MODEL)).reshape(ROWS_META, D_MODEL)
    x = jnp.concatenate([x_prompt.reshape(ROWS_P, D_MODEL), x_sample.reshape(ROWS_S, D_MODEL), meta,
                         jnp.zeros((ROWS_PAD, D_MODEL), F32)], axis=0)

    lb_p = jax.nn.softmax(hgrn_lb_raw, axis=0)
    lb_all = jnp.cumsum(lb_p, axis=0) - lb_p[0]
    zeros = lambda *s: jnp.zeros(s, F32)
    p_out = [[] for _ in range(6)]
    s_out = [[] for _ in range(6)]
    seg_p = (N_META, SEQ)
    seg_s = (DEC_SEQ,)

    for l in range(DEPTH):
        x = _ffn_half(x, ffn1_norm, ffn1_w_gate, ffn1_w_up, ffn1_w_down, l)
        u = _rmsnorm(x, mix_norm[l], BF16)
        if l % 2 == 0:
            e = l // 2
            proj = _matmul(u, even_w_in, e, IN_EVEN)
            proj_p, proj_s = _split_rows(proj)
            wts = (rglru_conv_w[e], rglru_conv_b[e], rglru_w_r[e], rglru_b_r[e], rglru_w_i[e], rglru_b_i[e],
                   rglru_lambda[e], lb_all[e], hgrn_norm[e])
            yp, cv, hh, ss = _jmix_even(proj_p, zeros(BATCH, CONV_W - 1, W_A), zeros(BATCH, W_A),
                                        zeros(BATCH, H_B, DK_B, DV_B), *wts, seg_p, True)
            for lst, val in zip(p_out[:3], (cv, hh, ss)):
                lst.append(val)
            ys, cv, hh, ss = _jmix_even(proj_s, state_rglru_conv[e], state_rglru_h[e], state_hgrn_S[e], *wts,
                                        seg_s, False)
            for lst, val in zip(s_out[:3], (cv, hh, ss)):
                lst.append(val)
            x = _matmul_residual(_join_rows(yp, ys), even_w_out, e, x, 1.0, TM)
        else:
            d = l // 2
            proj = _matmul(u, odd_w_in, d, IN_ODD_MAIN)
            w_gates = jnp.pad(odd_w_in[d][:, IN_ODD_MAIN:], ((0, 0), (0, LANES - 2 * H_C)))[None]
            gates = _matmul(u, w_gates, 0, LANES, tn=LANES)
            proj_p, proj_s = _split_rows(proj)
            gates_p, gates_s = _split_rows(gates)
            wts = (mlstm_b_gates[d], mlstm_norm[d])
            yp, cc, nn_, mm = _jmix_odd(proj_p, gates_p, zeros(BATCH, H_C, DK_C, DV_C), zeros(BATCH, H_C, DK_C),
                                        zeros(BATCH, H_C), *wts, seg_p)
            for lst, val in zip(p_out[3:], (cc, nn_, mm)):
                lst.append(val)
            ys, cc, nn_, mm = _jmix_odd(proj_s, gates_s, state_mlstm_C[d], state_mlstm_n[d], state_mlstm_m[d],
                                        *wts, seg_s)
            for lst, val in zip(s_out[3:], (cc, nn_, mm)):
                lst.append(val)
            x = _matmul_residual(_join_rows(yp, ys), odd_w_out, d, x, 1.0, TM)
        x = _ffn_half(x, ffn2_norm, ffn2_w_gate, ffn2_w_up, ffn2_w_down, l)

    y = _rmsnorm(x, final_norm, F32)
    y_prompt = y[:ROWS_P].reshape(BATCH, SEQ, D_MODEL)
    y_sample = y[ROW_S0:ROW_META0].reshape(DEC_BATCH, DEC_SEQ, D_MODEL)
    return (y_prompt, y_sample, *[jnp.stack(v) for v in p_out], *[jnp.stack(v) for v in s_out])
```

```python
import functools

import jax
import jax.numpy as jnp
from jax import lax
from jax.experimental import pallas as pl
from jax.experimental.pallas import tpu as pltpu

F32 = jnp.float32
BF16 = jnp.bfloat16

D_MODEL = 4096
BATCH = 4
SEQ = 2048
DEPTH = 4
DEC_BATCH = 128
DEC_SEQ = 4
N_META = 16
N_EVEN = (DEPTH + 1) // 2
N_ODD = DEPTH // 2
D_FF = 2 * D_MODEL
EPS = 1e-6
W_A = D_MODEL // 2
H_A = 16
BW_A = W_A // H_A
CONV_W = 4
RG_C = 8.0
W_B = D_MODEL // 2
DK_B = 128
H_B = W_B // DK_B
DV_B = W_B // H_B
F_B = H_B * DK_B
HGRN_CHUNK = 32
H_C = 8
DV_C = D_MODEL // H_C
DK_C = DV_C // 2
W_CV = H_C * DV_C
W_CK = H_C * DK_C
MLSTM_CHUNK = 128
IN_EVEN = 2 * W_A + 2 * F_B + 2 * W_B
IN_ODD_MAIN = 2 * W_CK + 2 * W_CV

ROWS_P = BATCH * SEQ
ROWS_S = DEC_BATCH * DEC_SEQ
ROWS_META = BATCH * N_META
ROW_S0 = ROWS_P
ROW_META0 = ROWS_P + ROWS_S
ROWS_USED = ROW_META0 + ROWS_META
ROWS = 8832
ROWS_PAD = ROWS - ROWS_USED

LANES = 128
SUB = 8
VMEM_LIMIT = 56 * 1024 * 1024

TM = 1472
TM_DOWN = 736
TN = 256
TR_NORM = 384


def _cparams(*sem):
    return pltpu.CompilerParams(dimension_semantics=sem, vmem_limit_bytes=VMEM_LIMIT)


def _rmsnorm_kernel(x_ref, g_ref, o_ref):
    x = x_ref[...]
    y = x * lax.rsqrt(jnp.mean(x * x, axis=-1, keepdims=True) + EPS)
    o_ref[...] = (y * g_ref[...]).astype(o_ref.dtype)


def _rmsnorm(x, g, out_dtype):
    rows, d = x.shape
    return pl.pallas_call(
        _rmsnorm_kernel,
        grid=(rows // TR_NORM,),
        in_specs=[pl.BlockSpec((TR_NORM, d), lambda i: (i, 0)),
                  pl.BlockSpec((1, d), lambda i: (0, 0))],
        out_specs=pl.BlockSpec((TR_NORM, d), lambda i: (i, 0)),
        out_shape=jax.ShapeDtypeStruct((rows, d), out_dtype),
        compiler_params=_cparams("parallel"),
        name="rmsnorm",
    )(x, g.reshape(1, d))


def _mm_kernel(a_ref, w_ref, o_ref):
    o_ref[...] = jnp.dot(a_ref[...], w_ref[...].astype(BF16), preferred_element_type=F32)


def _matmul(a, w3, layer, n_cols, tn=TN):
    rows, k = a.shape
    return pl.pallas_call(
        _mm_kernel,
        grid=(rows // TM, n_cols // tn),
        in_specs=[pl.BlockSpec((TM, k), lambda m, n: (m, 0)),
                  pl.BlockSpec((None, k, tn), lambda m, n: (layer, 0, n))],
        out_specs=pl.BlockSpec((TM, tn), lambda m, n: (m, n)),
        out_shape=jax.ShapeDtypeStruct((rows, n_cols), F32),
        compiler_params=_cparams("parallel", "arbitrary"),
        name="matmul",
    )(a, w3)


def _gateup_kernel(a_ref, wg_ref, wu_ref, o_ref):
    a = a_ref[...]
    g = jnp.dot(a, wg_ref[...].astype(BF16), preferred_element_type=F32)
    u = jnp.dot(a, wu_ref[...].astype(BF16), preferred_element_type=F32)
    o_ref[...] = (jax.nn.silu(g) * u).astype(o_ref.dtype)


def _gateup(a, wg, wu, layer):
    rows, k = a.shape
    n = wg.shape[-1]
    wspec = pl.BlockSpec((None, k, TN), lambda m, j: (layer, 0, j))
    return pl.pallas_call(
        _gateup_kernel,
        grid=(rows // TM, n // TN),
        in_specs=[pl.BlockSpec((TM, k), lambda m, j: (m, 0)), wspec, wspec],
        out_specs=pl.BlockSpec((TM, TN), lambda m, j: (m, j)),
        out_shape=jax.ShapeDtypeStruct((rows, n), BF16),
        compiler_params=_cparams("parallel", "arbitrary"),
        name="gateup",
    )(a, wg, wu)


def _mm_res_kernel(a_ref, w_ref, x_ref, o_ref, *, scale):
    acc = jnp.dot(a_ref[...], w_ref[...].astype(BF16), preferred_element_type=F32)
    o_ref[...] = x_ref[...] + scale * acc


def _matmul_residual(a, w3, layer, x, scale, tm):
    rows, k = a.shape
    n = w3.shape[-1]
    return pl.pallas_call(
        functools.partial(_mm_res_kernel, scale=scale),
        grid=(rows // tm, n // TN),
        in_specs=[pl.BlockSpec((tm, k), lambda m, j: (m, 0)),
                  pl.BlockSpec((None, k, TN), lambda m, j: (layer, 0, j)),
                  pl.BlockSpec((tm, TN), lambda m, j: (m, j))],
        out_specs=pl.BlockSpec((tm, TN), lambda m, j: (m, j)),
        out_shape=jax.ShapeDtypeStruct((rows, n), F32),
        input_output_aliases={2: 0},
        compiler_params=_cparams("parallel", "arbitrary"),
        name="matmul_residual",
    )(a, w3, x)


def _ffn_half(x, norm_g, wg, wu, wd, layer):
    xn = _rmsnorm(x, norm_g[layer], BF16)
    h = _gateup(xn, wg, wu, layer)
    return _matmul_residual(h, wd, layer, x, 0.5, TM_DOWN)


NEG = -1e30
_NT = (((1,), (1,)), ((), ()))
_TN = (((0,), (0,)), ((), ()))


def _dot(a, b, dims=None):
    if dims is None:
        return jnp.dot(a, b, preferred_element_type=F32)
    return lax.dot_general(a, b, dims, preferred_element_type=F32)


def _expm1_nonpos(x):
    u = jnp.exp(x)
    um1 = u - 1.0
    y = um1 * x / jnp.where(u == 1.0, 1.0, jnp.log(u))
    y = jnp.where(u == 1.0, x, y)
    return jnp.where(um1 == -1.0, -1.0, y)


def _rms_scale(x):
    return x * lax.rsqrt(jnp.mean(x * x, axis=-1, keepdims=True) + EPS)


def _cumsum_rows(x):
    rows = lax.broadcasted_iota(jnp.int32, x.shape, 0)
    d = 1
    while d < x.shape[0]:
        x = x + jnp.where(rows >= d, pltpu.roll(x, d, 0), 0.0)
        d *= 2
    return x


CG = 512
HPG = CG // BW_A
RG_ROWS = N_META + SEQ
RG_PAD = SUB
RG_CHUNK = 256


def _rglru_terms(xc, wr_ref, br_ref, wi_ref, bi_ref, lam_ref):
    xcb = xc.astype(BF16)
    rs, gs = [], []
    for h in range(HPG):
        xh = xcb[:, h * BW_A:(h + 1) * BW_A]
        rs.append(_dot(xh, wr_ref[h].astype(BF16)))
        gs.append(_dot(xh, wi_ref[h].astype(BF16)))
    r = jax.nn.sigmoid(jnp.concatenate(rs, axis=1) + br_ref[...])
    i = jax.nn.sigmoid(jnp.concatenate(gs, axis=1) + bi_ref[...])
    log_a = -RG_C * r * jax.nn.softplus(-lam_ref[...])
    a = jnp.exp(log_a)
    mult = jnp.sqrt(-_expm1_nonpos(2.0 * log_a))
    return a, mult, i


def _conv4(cw_ref, cb_ref, taps):
    y = cb_ref[...]
    for j in range(CONV_W):
        y = y + cw_ref[j:j + 1, :] * taps[j]
    return y


def _rglru_prompt_kernel(xm_ref, xe_ref, gm_ref, ge_ref, cw_ref, cb_ref, wr_ref, br_ref, wi_ref, bi_ref, lam_ref,
                         ym_ref, ye_ref, hl_ref, xs_scr, a_scr, b_scr):
    xs_scr[0:RG_PAD, :] = jnp.zeros((RG_PAD, CG), F32)
    xs_scr[RG_PAD:RG_PAD + N_META, :] = xe_ref[...]
    xs_scr[RG_PAD + N_META:, :] = xm_ref[...]
    chunks = [(0, N_META)] + [(N_META + j * RG_CHUNK, RG_CHUNK) for j in range(SEQ // RG_CHUNK)]
    for t0, n in chunks:
        taps = [xs_scr[RG_PAD + t0 - d:RG_PAD + t0 - d + n, :] for d in (3, 2, 1, 0)]
        xc = _conv4(cw_ref, cb_ref, taps)
        a, mult, i = _rglru_terms(xc, wr_ref, br_ref, wi_ref, bi_ref, lam_ref)
        if t0 == 0:
            mult = jnp.where(lax.broadcasted_iota(jnp.int32, (n, CG), 0) == 0, 1.0, mult)
        a_scr[t0:t0 + n, :] = a
        b_scr[t0:t0 + n, :] = mult * i * xc

    rows8 = lax.broadcasted_iota(jnp.int32, (SUB, CG), 0)

    def scan_block(blk, h_prev):
        r0 = pl.multiple_of(blk * SUB, SUB)
        a = a_scr[pl.ds(r0, SUB), :]
        b = b_scr[pl.ds(r0, SUB), :]
        for d in (1, 2, 4):
            keep = rows8 >= d
            b = jnp.where(keep, a * pltpu.roll(b, d, 0) + b, b)
            a = jnp.where(keep, a * pltpu.roll(a, d, 0), a)
        h = b + a * h_prev
        b_scr[pl.ds(r0, SUB), :] = h
        return h[SUB - 1:SUB, :]

    hl_ref[...] = lax.fori_loop(0, RG_ROWS // SUB, scan_block, jnp.zeros((1, CG), F32))
    ye_ref[...] = (b_scr[0:N_META, :] * jax.nn.gelu(ge_ref[...])).astype(BF16)
    for j in range(SEQ // RG_CHUNK):
        rows = slice(j * RG_CHUNK, (j + 1) * RG_CHUNK)
        h = b_scr[N_META + j * RG_CHUNK:N_META + (j + 1) * RG_CHUNK, :]
        ym_ref[rows, :] = (h * jax.nn.gelu(gm_ref[rows, :])).astype(BF16)


def _rglru_weight_specs(index):
    vec = pl.BlockSpec((1, CG), lambda *g: (0, index(*g)))
    gate = pl.BlockSpec((HPG, BW_A, BW_A), lambda *g: (index(*g), 0, 0))
    return [pl.BlockSpec((CONV_W, CG), lambda *g: (0, index(*g))), vec, gate, vec, gate, vec, vec]


def _rglru_weights(cw, cb, wr, br, wi, bi, lam):
    row = lambda a: a.reshape(1, W_A)
    return (cw, row(cb), wr, row(br), wi, row(bi), row(lam))


def _rglru_prompt(proj, weights):
    nm = ROW_META0 // N_META
    ng = W_A // CG
    return pl.pallas_call(
        _rglru_prompt_kernel,
        grid=(BATCH, ng),
        in_specs=[pl.BlockSpec((SEQ, CG), lambda b, c: (b, c)),
                  pl.BlockSpec((N_META, CG), lambda b, c: (nm + b, c)),
                  pl.BlockSpec((SEQ, CG), lambda b, c: (b, ng + c)),
                  pl.BlockSpec((N_META, CG), lambda b, c: (nm + b, ng + c))]
                 + _rglru_weight_specs(lambda b, c: c),
        out_specs=[pl.BlockSpec((SEQ, CG), lambda b, c: (b, c)),
                   pl.BlockSpec((N_META, CG), lambda b, c: (b, c)),
                   pl.BlockSpec((None, 1, CG), lambda b, c: (b, 0, c))],
        out_shape=[jax.ShapeDtypeStruct((ROWS_P, W_A), BF16),
                   jax.ShapeDtypeStruct((ROWS_META, W_A), BF16),
                   jax.ShapeDtypeStruct((BATCH, 1, W_A), F32)],
        scratch_shapes=[pltpu.VMEM((RG_PAD + RG_ROWS, CG), F32), pltpu.VMEM((RG_ROWS, CG), F32),
                        pltpu.VMEM((RG_ROWS, CG), F32)],
        compiler_params=_cparams("parallel", "parallel"),
        name="rglru_prompt",
    )(proj, proj, proj, proj, *weights)


def _rglru_sample_kernel(x_ref, g_ref, buf_ref, h0_ref, cw_ref, cb_ref, wr_ref, br_ref, wi_ref, bi_ref, lam_ref,
                         y_ref, hn_ref):
    xp = [buf_ref[j] for j in range(CONV_W - 1)] + [x_ref[t] for t in range(DEC_SEQ)]
    h = h0_ref[...]
    for t in range(DEC_SEQ):
        xc = _conv4(cw_ref, cb_ref, xp[t:t + CONV_W])
        a, mult, i = _rglru_terms(xc, wr_ref, br_ref, wi_ref, bi_ref, lam_ref)
        h = a * h + mult * i * xc
        y_ref[t] = (h * jax.nn.gelu(g_ref[t])).astype(BF16)
    hn_ref[...] = h


def _rglru_sample(xg_tm, buf_tm, h0, weights):
    ng = W_A // CG
    return pl.pallas_call(
        _rglru_sample_kernel,
        grid=(ng,),
        in_specs=[pl.BlockSpec((DEC_SEQ, DEC_BATCH, CG), lambda c: (0, 0, c)),
                  pl.BlockSpec((DEC_SEQ, DEC_BATCH, CG), lambda c: (0, 0, ng + c)),
                  pl.BlockSpec((CONV_W - 1, DEC_BATCH, CG), lambda c: (0, 0, c)),
                  pl.BlockSpec((DEC_BATCH, CG), lambda c: (0, c))]
                 + _rglru_weight_specs(lambda c: c),
        out_specs=[pl.BlockSpec((DEC_SEQ, DEC_BATCH, CG), lambda c: (0, 0, c)),
                   pl.BlockSpec((DEC_BATCH, CG), lambda c: (0, c))],
        out_shape=[jax.ShapeDtypeStruct((DEC_SEQ, DEC_BATCH, W_A), BF16),
                   jax.ShapeDtypeStruct((DEC_BATCH, W_A), F32)],
        compiler_params=_cparams("parallel"),
        name="rglru_sample",
    )(xg_tm, xg_tm, buf_tm, h0, *weights)


HG_ROWS = 128
DEC_PAD = SUB


def _lb_kernel(raw_ref, o_ref):
    raw = raw_ref[...]
    e = jnp.exp(raw - jnp.max(raw, axis=0, keepdims=True))
    p = e / jnp.sum(e, axis=0, keepdims=True)
    acc = jnp.zeros((1, F_B), F32)
    for r in range(N_EVEN):
        acc = acc + p[r:r + 1, :]
        o_ref[r:r + 1, :] = acc - p[0:1, :]


def _hgrn_lower_bounds(raw):
    return pl.pallas_call(_lb_kernel, out_shape=jax.ShapeDtypeStruct((N_EVEN, F_B), F32),
                          name="hgrn_lower_bounds")(raw)


def _hgrn_chunk(qb, fb, v, lb, st, n_valid):
    c = qb.shape[0]
    rows = lax.broadcasted_iota(jnp.int32, (c, DK_B), 0)
    f = lb + (1.0 - lb) * jax.nn.sigmoid(fb)
    kk = (1.0 - lb) * jax.nn.sigmoid(-fb)
    q = jax.nn.silu(qb)
    cum = _cumsum_rows(jnp.log(f))
    last = cum[n_valid - 1:n_valid, :]
    o_inter = _dot((q * jnp.exp(cum)).astype(BF16), st.astype(BF16), _NT)
    kdec = kk * jnp.exp(last - cum)
    if n_valid < c:
        kdec = jnp.where(rows < n_valid, kdec, 0.0)
    st_new = st * jnp.exp(last) + _dot(v.astype(BF16), kdec.astype(BF16), _TN)

    rows8 = lax.broadcasted_iota(jnp.int32, (SUB, DV_B), 0)
    blocks = []
    for blk in range(c // SUB):
        acc = jnp.zeros((SUB, DV_B), F32)
        n = (blk + 1) * SUB
        for r in range(SUB):
            t = blk * SUB + r
            if t >= n_valid:
                break
            d = jnp.where(rows[0:n] <= t, cum[t:t + 1, :] - cum[0:n, :], NEG)
            p = jnp.exp(d) * kk[0:n, :] * q[t:t + 1, :]
            o_t = jnp.sum(jnp.sum(p, axis=1, keepdims=True) * v[0:n, :], axis=0, keepdims=True)
            acc = jnp.where(rows8 == r, o_t, acc)
        blocks.append(acc)
    return jnp.concatenate(blocks, axis=0) + o_inter, st_new


def _hgrn_finish(o, nb_ref, g_ref, y_ref):
    y_ref[...] = (_rms_scale(o) * nb_ref[...] * jax.nn.silu(g_ref[...])).astype(y_ref.dtype)


def _hgrn_prompt_kernel(q_ref, f_ref, i_ref, g_ref, qe_ref, fe_ref, ie_ref, ge_ref, lb_ref, nb_ref,
                        ym_ref, ye_ref, s_ref, st_scr, o_scr):
    j = pl.program_id(1)

    def run_rows(qr, fr, ir, n_rows, chunk):
        def head(h, carry):
            lanes = pl.ds(pl.multiple_of(h * DK_B, DK_B), DK_B)
            lb = lb_ref[:, lanes]
            st = st_scr[h]
            for c0 in range(0, n_rows, chunk):
                rs = slice(c0, c0 + chunk)
                o, st = _hgrn_chunk(qr[rs, lanes], fr[rs, lanes], ir[rs, lanes], lb, st, chunk)
                o_scr[rs, lanes] = o
            st_scr[h] = st
            return carry
        lax.fori_loop(0, H_B, head, 0)

    @pl.when(j == 0)
    def _():
        st_scr[...] = jnp.zeros(st_scr.shape, F32)
        run_rows(qe_ref, fe_ref, ie_ref, N_META, N_META)
        _hgrn_finish(o_scr[0:N_META, :], nb_ref, ge_ref, ye_ref)

    run_rows(q_ref, f_ref, i_ref, HG_ROWS, HGRN_CHUNK)
    _hgrn_finish(o_scr[...], nb_ref, g_ref, ym_ref)

    @pl.when(j == pl.num_programs(1) - 1)
    def _():
        def head(h, carry):
            s_ref[h] = st_scr[h].T
            return carry
        lax.fori_loop(0, H_B, head, 0)


def _hgrn_prompt(proj, lb, norm_b):
    nm = ROW_META0 // N_META
    nj = SEQ // HG_ROWS
    main = lambda col: pl.BlockSpec((HG_ROWS, W_B), lambda b, j: (b * nj + j, col))
    meta = lambda col: pl.BlockSpec((N_META, W_B), lambda b, j: (nm + b, col))
    vec = pl.BlockSpec((1, W_B), lambda b, j: (0, 0))
    return pl.pallas_call(
        _hgrn_prompt_kernel,
        grid=(BATCH, nj),
        in_specs=[main(2), main(3), main(4), main(5), meta(2), meta(3), meta(4), meta(5), vec, vec],
        out_specs=[pl.BlockSpec((HG_ROWS, W_B), lambda b, j: (b * nj + j, 0)),
                   pl.BlockSpec((N_META, W_B), lambda b, j: (b, 0)),
                   pl.BlockSpec((None, H_B, DK_B, DV_B), lambda b, j: (b, 0, 0, 0))],
        out_shape=[jax.ShapeDtypeStruct((ROWS_P, W_B), BF16),
                   jax.ShapeDtypeStruct((ROWS_META, W_B), BF16),
                   jax.ShapeDtypeStruct((BATCH, H_B, DK_B, DV_B), F32)],
        scratch_shapes=[pltpu.VMEM((H_B, DV_B, DK_B), F32), pltpu.VMEM((HG_ROWS, W_B), F32)],
        compiler_params=_cparams("parallel", "arbitrary"),
        name="hgrn_prompt",
    )(*([proj] * 8), lb.reshape(1, F_B), norm_b.reshape(1, W_B))


def _hgrn_sample_kernel(q_ref, f_ref, i_ref, g_ref, lb_ref, nb_ref, s_in_ref, y_ref, s_out_ref, o_scr):
    def head(h, carry):
        lanes = pl.ds(pl.multiple_of(h * DK_B, DK_B), DK_B)
        o, st = _hgrn_chunk(q_ref[:, lanes], f_ref[:, lanes], i_ref[:, lanes], lb_ref[:, lanes], s_in_ref[h].T,
                            DEC_SEQ)
        o_scr[:, lanes] = o
        s_out_ref[h] = st.T
        return carry
    lax.fori_loop(0, H_B, head, 0)
    _hgrn_finish(o_scr[...], nb_ref, g_ref, y_ref)


def _hgrn_sample(proj_s, lb, norm_b, s0):
    part = lambda col: pl.BlockSpec((None, DEC_PAD, W_B), lambda b: (b, 0, col))
    vec = pl.BlockSpec((1, W_B), lambda b: (0, 0))
    state = pl.BlockSpec((None, H_B, DK_B, DV_B), lambda b: (b, 0, 0, 0))
    return pl.pallas_call(
        _hgrn_sample_kernel,
        grid=(DEC_BATCH,),
        in_specs=[part(2), part(3), part(4), part(5), vec, vec, state],
        out_specs=[pl.BlockSpec((None, DEC_PAD, W_B), lambda b: (b, 0, 0)), state],
        out_shape=[jax.ShapeDtypeStruct((DEC_BATCH, DEC_PAD, W_B), F32),
                   jax.ShapeDtypeStruct((DEC_BATCH, H_B, DK_B, DV_B), F32)],
        scratch_shapes=[pltpu.VMEM((DEC_PAD, W_B), F32)],
        compiler_params=_cparams("parallel"),
        name="hgrn_sample",
    )(*([proj_s] * 4), lb.reshape(1, F_B), norm_b.reshape(1, W_B), s0)


def _mlstm_chunk(q, k, v, ig, fg, C, n_row, m_prev, n_valid):
    c = q.shape[0]
    ti = lax.broadcasted_iota(jnp.int32, (c, c), 0)
    si = lax.broadcasted_iota(jnp.int32, (c, c), 1)
    tri = si <= ti
    to_row = lambda col: jnp.sum(jnp.where(ti == si, col, 0.0), axis=0, keepdims=True)
    cum = jnp.sum(jnp.where(tri, to_row(jax.nn.log_sigmoid(fg)), 0.0), axis=1, keepdims=True)
    logw = jnp.where(tri, cum - to_row(cum) + to_row(ig), NEG)
    log_inter = cum + m_prev
    m_t = jnp.maximum(log_inter, jnp.max(logw, axis=1, keepdims=True))
    w = jnp.exp(logw - m_t)
    g = jnp.exp(log_inter - m_t)
    qs = q * (DK_C ** -0.5)
    qb, vb = qs.astype(BF16), v.astype(BF16)
    p = _dot(qb, k.astype(BF16), _NT) * w
    num = g * _dot(qb, C.astype(BF16)) + _dot(p.astype(BF16), vb)
    den = g * jnp.sum(qs * n_row, axis=1, keepdims=True) + jnp.sum(p, axis=1, keepdims=True)
    h = num / jnp.maximum(jnp.abs(den), jnp.exp(-m_t))
    lv = n_valid - 1
    m_new = m_t[lv:lv + 1, :]
    cum_last = cum[lv:lv + 1, :]
    ws = jnp.exp(cum_last - cum + ig - m_new)
    if n_valid < c:
        ws = jnp.where(lax.broadcasted_iota(jnp.int32, (c, 1), 0) < n_valid, ws, 0.0)
    decay = jnp.exp(cum_last + m_prev - m_new)
    kw = k * ws
    C_new = decay * C + _dot(kw.astype(BF16), vb, _TN)
    n_new = decay * n_row + jnp.sum(kw, axis=0, keepdims=True)
    return h, C_new, n_new, m_new


def _mlstm_heads(q_ref, k_ref, v_ref, o_ref, g_ref, bias_ref, nc_ref, y_ref, get_state, put_state, n_valid):
    for h in range(H_C):
        kcols = slice(h * DK_C, (h + 1) * DK_C)
        vcols = slice(h * DV_C, (h + 1) * DV_C)
        ig = g_ref[:, h:h + 1] + bias_ref[0:1, h:h + 1]
        fg = g_ref[:, H_C + h:H_C + h + 1] + bias_ref[0:1, H_C + h:H_C + h + 1]
        C, n_row, m_prev = get_state(h)
        hh, C, n_row, m_new = _mlstm_chunk(q_ref[:, kcols], k_ref[:, kcols], v_ref[:, vcols], ig, fg, C, n_row,
                                           m_prev, n_valid)
        put_state(h, C, n_row, m_new)
        hn = _rms_scale(hh) * nc_ref[:, vcols]
        y_ref[:, vcols] = (jax.nn.sigmoid(o_ref[:, vcols]) * hn).astype(y_ref.dtype)


def _mlstm_prompt_kernel(q_ref, k_ref, v_ref, o_ref, g_ref, qe_ref, ke_ref, ve_ref, oe_ref, ge_ref, bias_ref, nc_ref,
                         ym_ref, ye_ref, c_ref, n_ref, m_ref):
    def get_state(h):
        return c_ref[h], n_ref[h:h + 1, :], m_ref[0:1, h:h + 1]

    def put_state(h, C, n_row, m_new):
        c_ref[h] = C
        n_ref[h:h + 1, :] = n_row
        m_ref[0:1, h:h + 1] = m_new

    @pl.when(pl.program_id(1) == 0)
    def _():
        c_ref[...] = jnp.zeros(c_ref.shape, F32)
        n_ref[...] = jnp.zeros(n_ref.shape, F32)
        m_ref[...] = jnp.zeros(m_ref.shape, F32)
        _mlstm_heads(qe_ref, ke_ref, ve_ref, oe_ref, ge_ref, bias_ref, nc_ref, ye_ref, get_state, put_state, N_META)

    _mlstm_heads(q_ref, k_ref, v_ref, o_ref, g_ref, bias_ref, nc_ref, ym_ref, get_state, put_state, MLSTM_CHUNK)


def _mlstm_prompt(proj, gates, bias, norm_c):
    nm = ROW_META0 // N_META
    nj = SEQ // MLSTM_CHUNK
    main = lambda width, col: pl.BlockSpec((MLSTM_CHUNK, width), lambda b, j: (b * nj + j, col))
    meta = lambda width, col: pl.BlockSpec((N_META, width), lambda b, j: (nm + b, col))
    return pl.pallas_call(
        _mlstm_prompt_kernel,
        grid=(BATCH, nj),
        in_specs=[main(W_CK, 0), main(W_CK, 1), main(W_CV, 1), main(W_CV, 2), main(LANES, 0),
                  meta(W_CK, 0), meta(W_CK, 1), meta(W_CV, 1), meta(W_CV, 2), meta(LANES, 0),
                  pl.BlockSpec((1, LANES), lambda b, j: (0, 0)), pl.BlockSpec((1, W_CV), lambda b, j: (0, 0))],
        out_specs=[pl.BlockSpec((MLSTM_CHUNK, W_CV), lambda b, j: (b * nj + j, 0)),
                   pl.BlockSpec((N_META, W_CV), lambda b, j: (b, 0)),
                   pl.BlockSpec((None, H_C, DK_C, DV_C), lambda b, j: (b, 0, 0, 0)),
                   pl.BlockSpec((None, H_C, DK_C), lambda b, j: (b, 0, 0)),
                   pl.BlockSpec((None, 1, H_C), lambda b, j: (b, 0, 0))],
        out_shape=[jax.ShapeDtypeStruct((ROWS_P, W_CV), BF16),
                   jax.ShapeDtypeStruct((ROWS_META, W_CV), BF16),
                   jax.ShapeDtypeStruct((BATCH, H_C, DK_C, DV_C), F32),
                   jax.ShapeDtypeStruct((BATCH, H_C, DK_C), F32),
                   jax.ShapeDtypeStruct((BATCH, 1, H_C), F32)],
        compiler_params=_cparams("parallel", "arbitrary"),
        name="mlstm_prompt",
    )(proj, proj, proj, proj, gates, proj, proj, proj, proj, gates, bias, norm_c.reshape(1, W_CV))


def _mlstm_sample_kernel(q_ref, k_ref, v_ref, o_ref, g_ref, bias_ref, nc_ref, c0_ref, n0_ref, m0_ref,
                         y_ref, c_ref, n_ref, m_ref):
    def get_state(h):
        return c0_ref[h], n0_ref[h:h + 1, :], m0_ref[0:1, h:h + 1]

    def put_state(h, C, n_row, m_new):
        c_ref[h] = C
        n_ref[h:h + 1, :] = n_row
        m_ref[0:1, h:h + 1] = m_new

    _mlstm_heads(q_ref, k_ref, v_ref, o_ref, g_ref, bias_ref, nc_ref, y_ref, get_state, put_state, DEC_SEQ)


def _mlstm_sample(proj_s, gates_s, bias, norm_c, c0, n0, m0):
    part = lambda width, col: pl.BlockSpec((None, DEC_PAD, width), lambda b: (b, 0, col))
    c_spec = pl.BlockSpec((None, H_C, DK_C, DV_C), lambda b: (b, 0, 0, 0))
    n_spec = pl.BlockSpec((None, H_C, DK_C), lambda b: (b, 0, 0))
    m_spec = pl.BlockSpec((None, 1, H_C), lambda b: (b, 0, 0))
    return pl.pallas_call(
        _mlstm_sample_kernel,
        grid=(DEC_BATCH,),
        in_specs=[part(W_CK, 0), part(W_CK, 1), part(W_CV, 1), part(W_CV, 2), part(LANES, 0),
                  pl.BlockSpec((1, LANES), lambda b: (0, 0)), pl.BlockSpec((1, W_CV), lambda b: (0, 0)),
                  c_spec, n_spec, m_spec],
        out_specs=[pl.BlockSpec((None, DEC_PAD, W_CV), lambda b: (b, 0, 0)), c_spec, n_spec, m_spec],
        out_shape=[jax.ShapeDtypeStruct((DEC_BATCH, DEC_PAD, W_CV), F32),
                   jax.ShapeDtypeStruct((DEC_BATCH, H_C, DK_C, DV_C), F32),
                   jax.ShapeDtypeStruct((DEC_BATCH, H_C, DK_C), F32),
                   jax.ShapeDtypeStruct((DEC_BATCH, 1, H_C), F32)],
        compiler_params=_cparams("parallel"),
        name="mlstm_sample",
    )(proj_s, proj_s, proj_s, proj_s, gates_s, bias, norm_c.reshape(1, W_CV), c0, n0, m0.reshape(DEC_BATCH, 1, H_C))


def _sample_rows(a):
    return a[ROW_S0:ROW_META0].reshape(DEC_BATCH, DEC_SEQ, a.shape[-1])


def _pad_steps(a):
    return jnp.pad(a, ((0, 0), (0, DEC_PAD - DEC_SEQ), (0, 0)))


def _join_rows(main, sample, meta):
    cat = lambda parts: parts[0] if len(parts) == 1 else jnp.concatenate(parts, axis=1)
    main, sample, meta = cat(main), cat(sample), cat(meta)
    return jnp.concatenate([main, sample, meta, jnp.zeros((ROWS_PAD, main.shape[1]), main.dtype)], axis=0)


def _mix_even(proj, conv0, h0, s0, rg_weights, lb, norm_b):
    ya_p, ya_e, h_p = _rglru_prompt(proj, rg_weights)
    yb_p, yb_e, s_p = _hgrn_prompt(proj, lb, norm_b)
    ps = _sample_rows(proj)
    xg_tm = jnp.swapaxes(ps[:, :, :2 * W_A], 0, 1)
    ya_s, h_s = _rglru_sample(xg_tm, jnp.swapaxes(conv0, 0, 1), h0, rg_weights)
    ya_s = jnp.swapaxes(ya_s, 0, 1).reshape(ROWS_S, W_A)
    yb_s, s_s = _hgrn_sample(_pad_steps(ps), lb, norm_b, s0)
    yb_s = yb_s[:, :DEC_SEQ].reshape(ROWS_S, W_B).astype(BF16)
    y = _join_rows([ya_p, yb_p], [ya_s, yb_s], [ya_e, yb_e])
    conv_p = proj[:ROWS_P].reshape(BATCH, SEQ, IN_EVEN)[:, SEQ - (CONV_W - 1):, :W_A]
    conv_s = ps[:, DEC_SEQ - (CONV_W - 1):, :W_A]
    return y, (conv_p, h_p.reshape(BATCH, W_A), s_p), (conv_s, h_s, s_s)


def _mix_odd(proj, gates, bias, norm_c, c0, n0, m0):
    y_p, y_e, c_p, n_p, m_p = _mlstm_prompt(proj, gates, bias, norm_c)
    y_s, c_s, n_s, m_s = _mlstm_sample(_pad_steps(_sample_rows(proj)), _pad_steps(_sample_rows(gates)), bias, norm_c,
                                       c0, n0, m0)
    y_s = y_s[:, :DEC_SEQ].reshape(ROWS_S, W_CV).astype(BF16)
    y = _join_rows([y_p], [y_s], [y_e])
    return y, (c_p, n_p, m_p.reshape(BATCH, H_C)), (c_s, n_s, m_s.reshape(DEC_BATCH, H_C))


def kernel(x_prompt, x_sample, state_rglru_conv, state_rglru_h, state_hgrn_S, state_mlstm_C, state_mlstm_n, state_mlstm_m, meta_tokens, ffn1_norm, ffn1_w_gate, ffn1_w_up, ffn1_w_down, mix_norm, ffn2_norm, ffn2_w_gate, ffn2_w_up, ffn2_w_down, even_w_in, rglru_conv_w, rglru_conv_b, rglru_w_r, rglru_b_r, rglru_w_i, rglru_b_i, rglru_lambda, hgrn_lb_raw, hgrn_norm, even_w_out, odd_w_in, mlstm_b_gates, mlstm_norm, odd_w_out, final_norm):
    meta = jnp.broadcast_to(meta_tokens[None], (BATCH, N_META, D_MODEL)).reshape(ROWS_META, D_MODEL)
    x = jnp.concatenate([x_prompt.reshape(ROWS_P, D_MODEL), x_sample.reshape(ROWS_S, D_MODEL), meta,
                         jnp.zeros((ROWS_PAD, D_MODEL), F32)], axis=0)
    lb_all = _hgrn_lower_bounds(hgrn_lb_raw)
    p_out = [[] for _ in range(6)]
    s_out = [[] for _ in range(6)]

    for l in range(DEPTH):
        x = _ffn_half(x, ffn1_norm, ffn1_w_gate, ffn1_w_up, ffn1_w_down, l)
        u = _rmsnorm(x, mix_norm[l], BF16)
        if l % 2 == 0:
            e = l // 2
            proj = _matmul(u, even_w_in, e, IN_EVEN)
            rg_weights = _rglru_weights(rglru_conv_w[e], rglru_conv_b[e], rglru_w_r[e], rglru_b_r[e], rglru_w_i[e],
                                        rglru_b_i[e], rglru_lambda[e])
            y, p_state, s_state = _mix_even(proj, state_rglru_conv[e], state_rglru_h[e], state_hgrn_S[e],
                                            rg_weights, lb_all[e], hgrn_norm[e])
            w_out, idx, first = even_w_out, e, 0
        else:
            d = l // 2
            proj = _matmul(u, odd_w_in, d, IN_ODD_MAIN)
            pad = ((0, 0), (0, LANES - 2 * H_C))
            gates = _matmul(u, jnp.pad(odd_w_in[d][:, IN_ODD_MAIN:], pad)[None], 0, LANES, tn=LANES)
            bias = jnp.pad(mlstm_b_gates[d].reshape(1, 2 * H_C), pad)
            y, p_state, s_state = _mix_odd(proj, gates, bias, mlstm_norm[d], state_mlstm_C[d], state_mlstm_n[d],
                                           state_mlstm_m[d])
            w_out, idx, first = odd_w_out, d, 3
        for i in range(3):
            p_out[first + i].append(p_state[i])
            s_out[first + i].append(s_state[i])
        x = _matmul_residual(y, w_out, idx, x, 1.0, TM)
        x = _ffn_half(x, ffn2_norm, ffn2_w_gate, ffn2_w_up, ffn2_w_down, l)

    y = _rmsnorm(x, final_norm, F32)
    y_prompt = y[:ROWS_P].reshape(BATCH, SEQ, D_MODEL)
    y_sample = y[ROW_S0:ROW_META0].reshape(DEC_BATCH, DEC_SEQ, D_MODEL)
    return (y_prompt, y_sample, *[jnp.stack(v) for v in p_out], *[jnp.stack(v) for v in s_out])
```

```python
import functools

import jax
import jax.numpy as jnp
from jax import lax
from jax.experimental import pallas as pl
from jax.experimental.pallas import tpu as pltpu

F32 = jnp.float32
BF16 = jnp.bfloat16

D_MODEL = 4096
BATCH = 4
SEQ = 2048
DEPTH = 4
DEC_BATCH = 128
DEC_SEQ = 4
N_META = 16
N_EVEN = (DEPTH + 1) // 2
N_ODD = DEPTH // 2
D_FF = 2 * D_MODEL
EPS = 1e-6
W_A = D_MODEL // 2
H_A = 16
BW_A = W_A // H_A
CONV_W = 4
RG_C = 8.0
W_B = D_MODEL // 2
DK_B = 128
H_B = W_B // DK_B
DV_B = W_B // H_B
F_B = H_B * DK_B
HGRN_CHUNK = 32
H_C = 8
DV_C = D_MODEL // H_C
DK_C = DV_C // 2
W_CV = H_C * DV_C
W_CK = H_C * DK_C
MLSTM_CHUNK = 128
IN_EVEN = 2 * W_A + 2 * F_B + 2 * W_B
IN_ODD_MAIN = 2 * W_CK + 2 * W_CV

ROWS_P = BATCH * SEQ
ROWS_S = DEC_BATCH * DEC_SEQ
ROWS_META = BATCH * N_META
ROW_S0 = ROWS_P
ROW_META0 = ROWS_P + ROWS_S
ROWS_USED = ROW_META0 + ROWS_META
ROWS = 8832
ROWS_PAD = ROWS - ROWS_USED

LANES = 128
SUB = 8
VMEM_LIMIT = 60 * 1024 * 1024

TM = 1472
TN = 256
TR_NORM = 384
TR_OUT = 512


def _cparams(*sem):
    return pltpu.CompilerParams(dimension_semantics=sem, vmem_limit_bytes=VMEM_LIMIT)


def _rmsnorm_kernel(x_ref, g_ref, o_ref):
    x = x_ref[...]
    y = x * lax.rsqrt(jnp.mean(x * x, axis=-1, keepdims=True) + EPS)
    o_ref[...] = (y * g_ref[...]).astype(o_ref.dtype)


def _rmsnorm(x, g, out_dtype, tile=TR_NORM, row0=0, rows=ROWS):
    d = x.shape[1]
    first = row0 // tile
    return pl.pallas_call(
        _rmsnorm_kernel,
        grid=(rows // tile,),
        in_specs=[pl.BlockSpec((tile, d), lambda i: (first + i, 0)),
                  pl.BlockSpec((1, d), lambda i: (0, 0))],
        out_specs=pl.BlockSpec((tile, d), lambda i: (i, 0)),
        out_shape=jax.ShapeDtypeStruct((rows, d), out_dtype),
        compiler_params=_cparams("parallel"),
        name="rmsnorm",
    )(x, g.reshape(1, d))


def _mm_kernel(a_ref, w_ref, o_ref):
    o_ref[...] = jnp.dot(a_ref[...], w_ref[...].astype(BF16), preferred_element_type=F32)


def _matmul(a, w3, layer, n_cols):
    rows, k = a.shape
    return pl.pallas_call(
        _mm_kernel,
        grid=(rows // TM, n_cols // TN),
        in_specs=[pl.BlockSpec((TM, k), lambda m, n: (m, 0)),
                  pl.BlockSpec((None, k, TN), lambda m, n: (layer, 0, n))],
        out_specs=pl.BlockSpec((TM, TN), lambda m, n: (m, n)),
        out_shape=jax.ShapeDtypeStruct((rows, n_cols), F32),
        compiler_params=_cparams("parallel", "arbitrary"),
        name="matmul",
    )(a, w3)


_NT = (((1,), (1,)), ((), ()))
_TN = (((0,), (0,)), ((), ()))


def _mm_t_kernel(a_ref, wt_ref, o_ref):
    o_ref[...] = lax.dot_general(a_ref[...], wt_ref[...].astype(BF16), _NT, preferred_element_type=F32)


def _matmul_t(a, w3t, layer, n_cols):
    rows, k = a.shape
    return pl.pallas_call(
        _mm_t_kernel,
        grid=(rows // TM, n_cols // TN),
        in_specs=[pl.BlockSpec((TM, k), lambda m, n: (m, 0)),
                  pl.BlockSpec((None, TN, k), lambda m, n: (layer, n, 0))],
        out_specs=pl.BlockSpec((TM, TN), lambda m, n: (m, n)),
        out_shape=jax.ShapeDtypeStruct((rows, n_cols), F32),
        compiler_params=_cparams("parallel", "arbitrary"),
        name="matmul_t",
    )(a, w3t)


def _mm_tail_kernel(a_ref, wt_ref, o_ref, *, n_valid):
    row = lax.broadcasted_iota(jnp.int32, wt_ref.shape, 0)
    wt = jnp.where(row < n_valid, wt_ref[...], 0.0).astype(BF16)
    o_ref[...] = lax.dot_general(a_ref[...], wt, _NT, preferred_element_type=F32)


def _matmul_tail(a, w3t, layer, row0):
    rows, k = a.shape
    n_valid = w3t.shape[1] - row0
    return pl.pallas_call(
        functools.partial(_mm_tail_kernel, n_valid=n_valid),
        grid=(rows // TM,),
        in_specs=[pl.BlockSpec((TM, k), lambda m: (m, 0)),
                  pl.BlockSpec((None, LANES, k), lambda m: (layer, row0 // LANES, 0))],
        out_specs=pl.BlockSpec((TM, LANES), lambda m: (m, 0)),
        out_shape=jax.ShapeDtypeStruct((rows, LANES), F32),
        compiler_params=_cparams("parallel"),
        name="matmul_tail",
    )(a, w3t)


def _gateup_kernel(a_ref, wg_ref, wu_ref, o_ref):
    a = a_ref[...]
    g = jnp.dot(a, wg_ref[...].astype(BF16), preferred_element_type=F32)
    u = jnp.dot(a, wu_ref[...].astype(BF16), preferred_element_type=F32)
    o_ref[...] = (jax.nn.silu(g) * u).astype(o_ref.dtype)


def _gateup(a, wg, wu, layer):
    rows, k = a.shape
    n = wg.shape[-1]
    wspec = pl.BlockSpec((None, k, TN), lambda m, j: (layer, 0, j))
    return pl.pallas_call(
        _gateup_kernel,
        grid=(rows // TM, n // TN),
        in_specs=[pl.BlockSpec((TM, k), lambda m, j: (m, 0)), wspec, wspec],
        out_specs=pl.BlockSpec((TM, TN), lambda m, j: (m, j)),
        out_shape=jax.ShapeDtypeStruct((rows, n), BF16),
        compiler_params=_cparams("parallel", "arbitrary"),
        name="gateup",
    )(a, wg, wu)


def _mm_res_kernel(a_ref, w_ref, x_ref, o_ref, *, scale):
    acc = jnp.dot(a_ref[...], w_ref[...].astype(BF16), preferred_element_type=F32)
    o_ref[...] = x_ref[...] + scale * acc


def _matmul_residual(a, w3, layer, x, scale):
    rows, k = a.shape
    n = w3.shape[-1]
    a_mode = dict(pipeline_mode=pl.Buffered(1)) if k > D_MODEL else {}
    return pl.pallas_call(
        functools.partial(_mm_res_kernel, scale=scale),
        grid=(rows // TM, n // TN),
        in_specs=[pl.BlockSpec((TM, k), lambda m, j: (m, 0), **a_mode),
                  pl.BlockSpec((None, k, TN), lambda m, j: (layer, 0, j)),
                  pl.BlockSpec((TM, TN), lambda m, j: (m, j))],
        out_specs=pl.BlockSpec((TM, TN), lambda m, j: (m, j)),
        out_shape=jax.ShapeDtypeStruct((rows, n), F32),
        input_output_aliases={2: 0},
        compiler_params=_cparams("parallel", "arbitrary"),
        name="matmul_residual",
    )(a, w3, x)


def _ffn_half(x, norm_g, wg, wu, wd, layer):
    xn = _rmsnorm(x, norm_g[layer], BF16)
    h = _gateup(xn, wg, wu, layer)
    return _matmul_residual(h, wd, layer, x, 0.5)


NEG = -1e30
LOG2E = 1.4426950408889634


def _dot(a, b, dims=None):
    if dims is None:
        return jnp.dot(a, b, preferred_element_type=F32)
    return lax.dot_general(a, b, dims, preferred_element_type=F32)


def _expm1_nonpos(x):
    u = jnp.exp(x)
    um1 = u - 1.0
    y = um1 * x / jnp.where(u == 1.0, 1.0, jnp.log(u))
    y = jnp.where(u == 1.0, x, y)
    return jnp.where(um1 == -1.0, -1.0, y)


def _rms_scale(x):
    return x * lax.rsqrt(jnp.mean(x * x, axis=-1, keepdims=True) + EPS)


def _passthrough(body, n_in, n_extra):
    def wrapped(*refs):
        return body(*refs[:n_in], *refs[n_in + n_extra:])
    return wrapped


_ANY = pl.BlockSpec(memory_space=pl.ANY)


def _cumsum_rows(x):
    rows = lax.broadcasted_iota(jnp.int32, x.shape, 0)
    d = 1
    while d < x.shape[0]:
        x = x + jnp.where(rows >= d, pltpu.roll(x, d, 0), 0.0)
        d *= 2
    return x


CG = 512
HPG = CG // BW_A
RG_ROWS = N_META + SEQ
RG_PAD = SUB
RG_CHUNK = 256


def _rglru_terms(xc, wr_ref, br_ref, wi_ref, bi_ref, lam_ref):
    xcb = xc.astype(BF16)
    rs, gs = [], []
    for h in range(HPG):
        xh = xcb[:, h * BW_A:(h + 1) * BW_A]
        rs.append(_dot(xh, wr_ref[h].astype(BF16)))
        gs.append(_dot(xh, wi_ref[h].astype(BF16)))
    r = jax.nn.sigmoid(jnp.concatenate(rs, axis=1) + br_ref[...])
    i = jax.nn.sigmoid(jnp.concatenate(gs, axis=1) + bi_ref[...])
    log_a = -RG_C * r * jax.nn.softplus(-lam_ref[...])
    a = jnp.exp(log_a)
    mult = jnp.sqrt(-_expm1_nonpos(2.0 * log_a))
    return a, mult, i


def _conv4(cw_ref, cb_ref, taps):
    y = cb_ref[...]
    for j in range(CONV_W):
        y = y + cw_ref[j:j + 1, :] * taps[j]
    return y


def _rglru_prompt_kernel(xm_ref, xe_ref, gm_ref, ge_ref, cw_ref, cb_ref, wr_ref, br_ref, wi_ref, bi_ref, lam_ref,
                         ym_ref, ye_ref, hl_ref, xs_scr, a_scr, b_scr):
    xs_scr[0:RG_PAD, :] = jnp.zeros((RG_PAD, CG), F32)
    xs_scr[RG_PAD:RG_PAD + N_META, :] = xe_ref[...]
    xs_scr[RG_PAD + N_META:, :] = xm_ref[...]
    chunks = [(0, N_META)] + [(N_META + j * RG_CHUNK, RG_CHUNK) for j in range(SEQ // RG_CHUNK)]
    for t0, n in chunks:
        taps = [xs_scr[RG_PAD + t0 - d:RG_PAD + t0 - d + n, :] for d in (3, 2, 1, 0)]
        xc = _conv4(cw_ref, cb_ref, taps)
        a, mult, i = _rglru_terms(xc, wr_ref, br_ref, wi_ref, bi_ref, lam_ref)
        if t0 == 0:
            mult = jnp.where(lax.broadcasted_iota(jnp.int32, (n, CG), 0) == 0, 1.0, mult)
        a_scr[t0:t0 + n, :] = a
        b_scr[t0:t0 + n, :] = mult * i * xc

    rows8 = lax.broadcasted_iota(jnp.int32, (SUB, CG), 0)

    def scan_block(blk, h_prev):
        r0 = pl.multiple_of(blk * SUB, SUB)
        a = a_scr[pl.ds(r0, SUB), :]
        b = b_scr[pl.ds(r0, SUB), :]
        for d in (1, 2, 4):
            keep = rows8 >= d
            b = jnp.where(keep, a * pltpu.roll(b, d, 0) + b, b)
            a = jnp.where(keep, a * pltpu.roll(a, d, 0), a)
        h = b + a * h_prev
        b_scr[pl.ds(r0, SUB), :] = h
        return h[SUB - 1:SUB, :]

    hl_ref[...] = lax.fori_loop(0, RG_ROWS // SUB, scan_block, jnp.zeros((1, CG), F32))
    ye_ref[...] = (b_scr[0:N_META, :] * jax.nn.gelu(ge_ref[...])).astype(BF16)
    for j in range(SEQ // RG_CHUNK):
        rows = slice(j * RG_CHUNK, (j + 1) * RG_CHUNK)
        h = b_scr[N_META + j * RG_CHUNK:N_META + (j + 1) * RG_CHUNK, :]
        ym_ref[rows, :] = (h * jax.nn.gelu(gm_ref[rows, :])).astype(BF16)


def _rglru_weight_specs(index):
    vec = pl.BlockSpec((1, CG), lambda *g: (0, index(*g)))
    gate = pl.BlockSpec((HPG, BW_A, BW_A), lambda *g: (index(*g), 0, 0))
    return [pl.BlockSpec((CONV_W, CG), lambda *g: (0, index(*g))), vec, gate, vec, gate, vec, vec]


def _rglru_weights(cw, cb, wr, br, wi, bi, lam):
    row = lambda a: a.reshape(1, W_A)
    return (cw, row(cb), wr, row(br), wi, row(bi), row(lam))


def _rglru_prompt(proj, weights, e, h_prev):
    nm = ROW_META0 // N_META
    ng = W_A // CG
    n_in = 11
    extra = [] if h_prev is None else [h_prev]
    return pl.pallas_call(
        _passthrough(_rglru_prompt_kernel, n_in, len(extra)),
        grid=(BATCH, ng),
        in_specs=[pl.BlockSpec((SEQ, CG), lambda b, c: (b, c)),
                  pl.BlockSpec((N_META, CG), lambda b, c: (nm + b, c)),
                  pl.BlockSpec((SEQ, CG), lambda b, c: (b, ng + c)),
                  pl.BlockSpec((N_META, CG), lambda b, c: (nm + b, ng + c))]
                 + _rglru_weight_specs(lambda b, c: c) + [_ANY] * len(extra),
        out_specs=[pl.BlockSpec((SEQ, CG), lambda b, c: (b, c)),
                   pl.BlockSpec((N_META, CG), lambda b, c: (b, c)),
                   pl.BlockSpec((None, None, 1, CG), lambda b, c: (e, b, 0, c))],
        out_shape=[jax.ShapeDtypeStruct((ROWS, D_MODEL), BF16),
                   jax.ShapeDtypeStruct((ROWS_META, W_A), BF16),
                   jax.ShapeDtypeStruct((N_EVEN, BATCH, 1, W_A), F32)],
        scratch_shapes=[pltpu.VMEM((RG_PAD + RG_ROWS, CG), F32), pltpu.VMEM((RG_ROWS, CG), F32),
                        pltpu.VMEM((RG_ROWS, CG), F32)],
        input_output_aliases={n_in + i: 2 + i for i in range(len(extra))},
        compiler_params=_cparams("parallel", "parallel"),
        name="rglru_prompt",
    )(proj, proj, proj, proj, *weights, *extra)


def _rglru_sample_kernel(x_ref, g_ref, buf_ref, h0_ref, cw_ref, cb_ref, wr_ref, br_ref, wi_ref, bi_ref, lam_ref,
                         y_ref, hn_ref):
    xp = [buf_ref[j] for j in range(CONV_W - 1)] + [x_ref[t] for t in range(DEC_SEQ)]
    h = h0_ref[...]
    for t in range(DEC_SEQ):
        xc = _conv4(cw_ref, cb_ref, xp[t:t + CONV_W])
        a, mult, i = _rglru_terms(xc, wr_ref, br_ref, wi_ref, bi_ref, lam_ref)
        h = a * h + mult * i * xc
        y_ref[t] = (h * jax.nn.gelu(g_ref[t])).astype(BF16)
    hn_ref[...] = h


def _rglru_sample(xg_tm, buf_tm, h_all, weights, e, h_prev):
    ng = W_A // CG
    n_in = 11
    extra = [] if h_prev is None else [h_prev]
    state = pl.BlockSpec((None, DEC_BATCH, CG), lambda c: (e, 0, c))
    return pl.pallas_call(
        _passthrough(_rglru_sample_kernel, n_in, len(extra)),
        grid=(ng,),
        in_specs=[pl.BlockSpec((DEC_SEQ, DEC_BATCH, CG), lambda c: (0, 0, c)),
                  pl.BlockSpec((DEC_SEQ, DEC_BATCH, CG), lambda c: (0, 0, ng + c)),
                  pl.BlockSpec((CONV_W - 1, DEC_BATCH, CG), lambda c: (0, 0, c)),
                  state]
                 + _rglru_weight_specs(lambda c: c) + [_ANY] * len(extra),
        out_specs=[pl.BlockSpec((DEC_SEQ, DEC_BATCH, CG), lambda c: (0, 0, c)), state],
        out_shape=[jax.ShapeDtypeStruct((DEC_SEQ, DEC_BATCH, W_A), BF16),
                   jax.ShapeDtypeStruct((N_EVEN, DEC_BATCH, W_A), F32)],
        input_output_aliases={n_in + i: 1 + i for i in range(len(extra))},
        compiler_params=_cparams("parallel"),
        name="rglru_sample",
    )(xg_tm, xg_tm, buf_tm, h_all, *weights, *extra)


HG_ROWS = 128
DEC_PAD = SUB


def _lb_kernel(raw_ref, o_ref):
    raw = raw_ref[...]
    e = jnp.exp(raw - jnp.max(raw, axis=0, keepdims=True))
    p = e / jnp.sum(e, axis=0, keepdims=True)
    acc = jnp.zeros((1, F_B), F32)
    for r in range(N_EVEN):
        acc = acc + p[r:r + 1, :]
        o_ref[r:r + 1, :] = acc - p[0:1, :]


def _hgrn_lower_bounds(raw):
    return pl.pallas_call(_lb_kernel, out_shape=jax.ShapeDtypeStruct((N_EVEN, F_B), F32),
                          name="hgrn_lower_bounds")(raw)


def _hgrn_chunk(qb, fb, v, lb, st, n_valid):
    c = qb.shape[0]
    rows = lax.broadcasted_iota(jnp.int32, (c, DK_B), 0)
    f = lb + (1.0 - lb) * jax.nn.sigmoid(fb)
    kk = (1.0 - lb) * jax.nn.sigmoid(-fb)
    q = jax.nn.silu(qb)
    cum = _cumsum_rows(jnp.log(f))
    last = cum[n_valid - 1:n_valid, :]
    o_inter = _dot((q * jnp.exp(cum)).astype(BF16), st.astype(BF16), _NT)
    kdec = kk * jnp.exp(last - cum)
    if n_valid < c:
        kdec = jnp.where(rows < n_valid, kdec, 0.0)
    st_new = st * jnp.exp(last) + _dot(v.astype(BF16), kdec.astype(BF16), _TN)

    rows8 = lax.broadcasted_iota(jnp.int32, (SUB, DK_B), 0)
    cum2 = cum * LOG2E
    src2 = cum2 - jnp.log2(kk)
    o = o_inter
    for s in range(n_valid):
        t0 = (s // SUB) * SUB
        d = cum2[t0:c, :] - src2[s:s + 1, :]
        head = jnp.where(rows8 >= s - t0, d[0:SUB, :], NEG)
        d = head if t0 + SUB == c else jnp.concatenate([head, d[SUB:, :]], axis=0)
        p = jnp.exp2(d) * q[t0:c, :]
        term = jnp.sum(p, axis=1, keepdims=True) * v[s:s + 1, :]
        o = term + o if t0 == 0 else jnp.concatenate([o[0:t0, :], term + o[t0:c, :]], axis=0)
    return o, st_new


def _hgrn_finish(o, nb_ref, g_ref, y_ref):
    y_ref[...] = (_rms_scale(o) * nb_ref[...] * jax.nn.silu(g_ref[...])).astype(y_ref.dtype)


def _hgrn_prompt_kernel(q_ref, f_ref, i_ref, g_ref, qe_ref, fe_ref, ie_ref, ge_ref, lb_ref, nb_ref,
                        ym_ref, ye_ref, s_ref, st_scr, o_scr):
    j = pl.program_id(1)

    def run_rows(qr, fr, ir, n_rows, chunk):
        def head(h, carry):
            lanes = pl.ds(pl.multiple_of(h * DK_B, DK_B), DK_B)
            lb = lb_ref[:, lanes]
            st = st_scr[h]
            for c0 in range(0, n_rows, chunk):
                rs = slice(c0, c0 + chunk)
                o, st = _hgrn_chunk(qr[rs, lanes], fr[rs, lanes], ir[rs, lanes], lb, st, chunk)
                o_scr[rs, lanes] = o
            st_scr[h] = st
            return carry
        lax.fori_loop(0, H_B, head, 0)

    @pl.when(j == 0)
    def _():
        st_scr[...] = jnp.zeros(st_scr.shape, F32)
        run_rows(qe_ref, fe_ref, ie_ref, N_META, N_META)
        _hgrn_finish(o_scr[0:N_META, :], nb_ref, ge_ref, ye_ref)

    run_rows(q_ref, f_ref, i_ref, HG_ROWS, HGRN_CHUNK)
    _hgrn_finish(o_scr[...], nb_ref, g_ref, ym_ref)

    @pl.when(j == pl.num_programs(1) - 1)
    def _():
        def head(h, carry):
            s_ref[h] = st_scr[h].T
            return carry
        lax.fori_loop(0, H_B, head, 0)


def _hgrn_prompt(proj, y, lb, norm_b, e, s_prev):
    nm = ROW_META0 // N_META
    nj = SEQ // HG_ROWS
    main = lambda col: pl.BlockSpec((HG_ROWS, W_B), lambda b, j: (b * nj + j, col))
    meta = lambda col: pl.BlockSpec((N_META, W_B), lambda b, j: (nm + b, col))
    vec = pl.BlockSpec((1, W_B), lambda b, j: (0, 0))
    n_in = 10
    extra = [y] + ([] if s_prev is None else [s_prev])
    aliases = {n_in: 0}
    if s_prev is not None:
        aliases[n_in + 1] = 2
    return pl.pallas_call(
        _passthrough(_hgrn_prompt_kernel, n_in, len(extra)),
        grid=(BATCH, nj),
        in_specs=[main(2), main(3), main(4), main(5), meta(2), meta(3), meta(4), meta(5), vec, vec]
                 + [_ANY] * len(extra),
        out_specs=[pl.BlockSpec((HG_ROWS, W_B), lambda b, j: (b * nj + j, W_A // W_B)),
                   pl.BlockSpec((N_META, W_B), lambda b, j: (b, 0)),
                   pl.BlockSpec((None, None, H_B, DK_B, DV_B), lambda b, j: (e, b, 0, 0, 0))],
        out_shape=[jax.ShapeDtypeStruct((ROWS, D_MODEL), BF16),
                   jax.ShapeDtypeStruct((ROWS_META, W_B), BF16),
                   jax.ShapeDtypeStruct((N_EVEN, BATCH, H_B, DK_B, DV_B), F32)],
        scratch_shapes=[pltpu.VMEM((H_B, DV_B, DK_B), F32), pltpu.VMEM((HG_ROWS, W_B), F32)],
        input_output_aliases=aliases,
        compiler_params=_cparams("parallel", "arbitrary"),
        name="hgrn_prompt",
    )(*([proj] * 8), lb.reshape(1, F_B), norm_b.reshape(1, W_B), *extra)


HG_UNROLL = 4


def _hgrn_sample_kernel(q_ref, f_ref, i_ref, g_ref, lb_ref, nb_ref, s_in_ref, y_ref, s_out_ref, o_scr):
    def head(h, carry):
        lanes = pl.ds(pl.multiple_of(h * DK_B, DK_B), DK_B)
        o, st = _hgrn_chunk(q_ref[:, lanes], f_ref[:, lanes], i_ref[:, lanes], lb_ref[:, lanes], s_in_ref[h].T,
                            DEC_SEQ)
        o_scr[:, lanes] = o
        s_out_ref[h] = st.T
        return carry
    lax.fori_loop(0, H_B, head, 0, unroll=HG_UNROLL)
    _hgrn_finish(o_scr[...], nb_ref, g_ref, y_ref)


def _hgrn_sample(proj_s, lb, norm_b, s_all, e, s_prev):
    part = lambda col: pl.BlockSpec((None, DEC_PAD, W_B), lambda b: (b, 0, col))
    vec = pl.BlockSpec((1, W_B), lambda b: (0, 0))
    state = pl.BlockSpec((None, None, H_B, DK_B, DV_B), lambda b: (e, b, 0, 0, 0))
    n_in = 7
    extra = [] if s_prev is None else [s_prev]
    return pl.pallas_call(
        _passthrough(_hgrn_sample_kernel, n_in, len(extra)),
        grid=(DEC_BATCH,),
        in_specs=[part(2), part(3), part(4), part(5), vec, vec, state] + [_ANY] * len(extra),
        out_specs=[pl.BlockSpec((None, DEC_PAD, W_B), lambda b: (b, 0, 0)), state],
        out_shape=[jax.ShapeDtypeStruct((DEC_BATCH, DEC_PAD, W_B), F32),
                   jax.ShapeDtypeStruct((N_EVEN, DEC_BATCH, H_B, DK_B, DV_B), F32)],
        scratch_shapes=[pltpu.VMEM((DEC_PAD, W_B), F32)],
        input_output_aliases={n_in + i: 1 + i for i in range(len(extra))},
        compiler_params=_cparams("parallel"),
        name="hgrn_sample",
    )(*([proj_s] * 4), lb.reshape(1, F_B), norm_b.reshape(1, W_B), s_all, *extra)


def _mlstm_chunk(q, k, v, ig, fg, C, n_row, m_prev, n_valid):
    c = q.shape[0]
    ti = lax.broadcasted_iota(jnp.int32, (c, c), 0)
    si = lax.broadcasted_iota(jnp.int32, (c, c), 1)
    tri = si <= ti
    to_row = lambda col: jnp.sum(jnp.where(ti == si, col, 0.0), axis=0, keepdims=True)
    cum = jnp.sum(jnp.where(tri, to_row(jax.nn.log_sigmoid(fg)), 0.0), axis=1, keepdims=True)
    logw = jnp.where(tri, cum - to_row(cum) + to_row(ig), NEG)
    log_inter = cum + m_prev
    m_t = jnp.maximum(log_inter, jnp.max(logw, axis=1, keepdims=True))
    w = jnp.exp(logw - m_t)
    g = jnp.exp(log_inter - m_t)
    qs = q * (DK_C ** -0.5)
    qb, vb = qs.astype(BF16), v.astype(BF16)
    p = _dot(qb, k.astype(BF16), _NT) * w
    num = g * _dot(qb, C.astype(BF16)) + _dot(p.astype(BF16), vb)
    den = g * jnp.sum(qs * n_row, axis=1, keepdims=True) + jnp.sum(p, axis=1, keepdims=True)
    h = num / jnp.maximum(jnp.abs(den), jnp.exp(-m_t))
    lv = n_valid - 1
    m_new = m_t[lv:lv + 1, :]
    cum_last = cum[lv:lv + 1, :]
    ws = jnp.exp(cum_last - cum + ig - m_new)
    if n_valid < c:
        ws = jnp.where(lax.broadcasted_iota(jnp.int32, (c, 1), 0) < n_valid, ws, 0.0)
    decay = jnp.exp(cum_last + m_prev - m_new)
    kw = k * ws
    C_new = decay * C + _dot(kw.astype(BF16), vb, _TN)
    n_new = decay * n_row + jnp.sum(kw, axis=0, keepdims=True)
    return h, C_new, n_new, m_new


def _mlstm_heads(q_ref, k_ref, v_ref, o_ref, g_ref, bias_ref, nc_ref, y_ref, get_state, put_state, n_valid):
    for h in range(H_C):
        kcols = slice(h * DK_C, (h + 1) * DK_C)
        vcols = slice(h * DV_C, (h + 1) * DV_C)
        ig = g_ref[:, h:h + 1] + bias_ref[0:1, h:h + 1]
        fg = g_ref[:, H_C + h:H_C + h + 1] + bias_ref[0:1, H_C + h:H_C + h + 1]
        C, n_row, m_prev = get_state(h)
        hh, C, n_row, m_new = _mlstm_chunk(q_ref[:, kcols], k_ref[:, kcols], v_ref[:, vcols], ig, fg, C, n_row,
                                           m_prev, n_valid)
        put_state(h, C, n_row, m_new)
        hn = _rms_scale(hh) * nc_ref[:, vcols]
        y_ref[:, vcols] = (jax.nn.sigmoid(o_ref[:, vcols]) * hn).astype(y_ref.dtype)


def _mlstm_prompt_kernel(q_ref, k_ref, v_ref, o_ref, g_ref, qe_ref, ke_ref, ve_ref, oe_ref, ge_ref, bias_ref, nc_ref,
                         ym_ref, ye_ref, c_ref, n_ref, m_ref):
    def get_state(h):
        return c_ref[h], n_ref[h:h + 1, :], m_ref[0:1, h:h + 1]

    def put_state(h, C, n_row, m_new):
        c_ref[h] = C
        n_ref[h:h + 1, :] = n_row
        m_ref[0:1, h:h + 1] = m_new

    @pl.when(pl.program_id(1) == 0)
    def _():
        c_ref[...] = jnp.zeros(c_ref.shape, F32)
        n_ref[...] = jnp.zeros(n_ref.shape, F32)
        m_ref[...] = jnp.zeros(m_ref.shape, F32)
        _mlstm_heads(qe_ref, ke_ref, ve_ref, oe_ref, ge_ref, bias_ref, nc_ref, ye_ref, get_state, put_state, N_META)

    _mlstm_heads(q_ref, k_ref, v_ref, o_ref, g_ref, bias_ref, nc_ref, ym_ref, get_state, put_state, MLSTM_CHUNK)


def _mlstm_state_specs(d, index):
    return [pl.BlockSpec((None, None, H_C, DK_C, DV_C), lambda *g: (d, index(*g), 0, 0, 0)),
            pl.BlockSpec((None, None, H_C, DK_C), lambda *g: (d, index(*g), 0, 0)),
            pl.BlockSpec((None, None, 1, H_C), lambda *g: (d, index(*g), 0, 0))]


def _mlstm_state_shapes(batch):
    return [jax.ShapeDtypeStruct((N_ODD, batch, H_C, DK_C, DV_C), F32),
            jax.ShapeDtypeStruct((N_ODD, batch, H_C, DK_C), F32),
            jax.ShapeDtypeStruct((N_ODD, batch, 1, H_C), F32)]


def _mlstm_prompt(proj, gates, bias, norm_c, d, prev):
    nm = ROW_META0 // N_META
    nj = SEQ // MLSTM_CHUNK
    main = lambda width, col: pl.BlockSpec((MLSTM_CHUNK, width), lambda b, j: (b * nj + j, col))
    meta = lambda width, col: pl.BlockSpec((N_META, width), lambda b, j: (nm + b, col))
    n_in = 12
    extra = [] if prev is None else list(prev)
    return pl.pallas_call(
        _passthrough(_mlstm_prompt_kernel, n_in, len(extra)),
        grid=(BATCH, nj),
        in_specs=[main(W_CK, 0), main(W_CK, 1), main(W_CV, 1), main(W_CV, 2), main(LANES, 0),
                  meta(W_CK, 0), meta(W_CK, 1), meta(W_CV, 1), meta(W_CV, 2), meta(LANES, 0),
                  pl.BlockSpec((1, LANES), lambda b, j: (0, 0)), pl.BlockSpec((1, W_CV), lambda b, j: (0, 0))]
                 + [_ANY] * len(extra),
        out_specs=[pl.BlockSpec((MLSTM_CHUNK, W_CV), lambda b, j: (b * nj + j, 0)),
                   pl.BlockSpec((N_META, W_CV), lambda b, j: (b, 0))]
                  + _mlstm_state_specs(d, lambda b, j: b),
        out_shape=[jax.ShapeDtypeStruct((ROWS, W_CV), BF16),
                   jax.ShapeDtypeStruct((ROWS_META, W_CV), BF16)] + _mlstm_state_shapes(BATCH),
        input_output_aliases={n_in + i: 2 + i for i in range(len(extra))},
        compiler_params=_cparams("parallel", "arbitrary"),
        name="mlstm_prompt",
    )(proj, proj, proj, proj, gates, proj, proj, proj, proj, gates, bias, norm_c.reshape(1, W_CV), *extra)


def _mlstm_sample_kernel(q_ref, k_ref, v_ref, o_ref, g_ref, bias_ref, nc_ref, c0_ref, n0_ref, m0_ref,
                         y_ref, c_ref, n_ref, m_ref):
    def get_state(h):
        return c0_ref[h], n0_ref[h:h + 1, :], m0_ref[0:1, h:h + 1]

    def put_state(h, C, n_row, m_new):
        c_ref[h] = C
        n_ref[h:h + 1, :] = n_row
        m_ref[0:1, h:h + 1] = m_new

    _mlstm_heads(q_ref, k_ref, v_ref, o_ref, g_ref, bias_ref, nc_ref, y_ref, get_state, put_state, DEC_SEQ)


def _mlstm_sample(proj_s, gates_s, bias, norm_c, c_all, n_all, m_all, d, prev):
    part = lambda width, col: pl.BlockSpec((None, DEC_PAD, width), lambda b: (b, 0, col))
    states = _mlstm_state_specs(d, lambda b: b)
    n_in = 10
    extra = [] if prev is None else list(prev)
    return pl.pallas_call(
        _passthrough(_mlstm_sample_kernel, n_in, len(extra)),
        grid=(DEC_BATCH,),
        in_specs=[part(W_CK, 0), part(W_CK, 1), part(W_CV, 1), part(W_CV, 2), part(LANES, 0),
                  pl.BlockSpec((1, LANES), lambda b: (0, 0)), pl.BlockSpec((1, W_CV), lambda b: (0, 0))]
                 + states + [_ANY] * len(extra),
        out_specs=[pl.BlockSpec((None, DEC_PAD, W_CV), lambda b: (b, 0, 0))] + states,
        out_shape=[jax.ShapeDtypeStruct((DEC_BATCH, DEC_PAD, W_CV), F32)] + _mlstm_state_shapes(DEC_BATCH),
        input_output_aliases={n_in + i: 1 + i for i in range(len(extra))},
        compiler_params=_cparams("parallel"),
        name="mlstm_sample",
    )(proj_s, proj_s, proj_s, proj_s, gates_s, bias, norm_c.reshape(1, W_CV), c_all, n_all, m_all, *extra)


def _sample_rows(a):
    return a[ROW_S0:ROW_META0].reshape(DEC_BATCH, DEC_SEQ, a.shape[-1])


def _pad_steps(a):
    return jnp.pad(a, ((0, 0), (0, DEC_PAD - DEC_SEQ), (0, 0)))


def _fill_tail_rows(y, sample, meta):
    cat = lambda parts: parts[0] if len(parts) == 1 else jnp.concatenate(parts, axis=1)
    tail = jnp.concatenate([cat(sample), cat(meta), jnp.zeros((ROWS_PAD, y.shape[1]), y.dtype)], axis=0)
    return lax.dynamic_update_slice(y, tail, (ROWS_P, 0))


def _mix_even(proj, e, conv_all, h_all, s_all, rg_weights, lb, norm_b, prev):
    p_h0, p_s0, s_h0, s_s0 = prev if prev is not None else (None,) * 4
    y, ya_e, p_h = _rglru_prompt(proj, rg_weights, e, p_h0)
    y, yb_e, p_s = _hgrn_prompt(proj, y, lb, norm_b, e, p_s0)
    ps = _sample_rows(proj)
    xg_tm = jnp.swapaxes(ps[:, :, :2 * W_A], 0, 1)
    ya_s, s_h = _rglru_sample(xg_tm, jnp.swapaxes(conv_all[e], 0, 1), h_all, rg_weights, e, s_h0)
    ya_s = jnp.swapaxes(ya_s, 0, 1).reshape(ROWS_S, W_A)
    yb_s, s_s = _hgrn_sample(_pad_steps(ps), lb, norm_b, s_all, e, s_s0)
    yb_s = yb_s[:, :DEC_SEQ].reshape(ROWS_S, W_B).astype(BF16)
    y = _fill_tail_rows(y, [ya_s, yb_s], [ya_e, yb_e])
    conv_p = jnp.stack([proj[(b + 1) * SEQ - (CONV_W - 1):(b + 1) * SEQ, :W_A] for b in range(BATCH)])
    conv_s = ps[:, DEC_SEQ - (CONV_W - 1):, :W_A]
    return y, conv_p, conv_s, (p_h, p_s, s_h, s_s)


def _mix_odd(proj, gates, bias, norm_c, d, c_all, n_all, m_all, prev):
    p_prev, s_prev = (prev[:3], prev[3:]) if prev is not None else (None, None)
    y, y_e, *p_state = _mlstm_prompt(proj, gates, bias, norm_c, d, p_prev)
    y_s, *s_state = _mlstm_sample(_pad_steps(_sample_rows(proj)), _pad_steps(_sample_rows(gates)), bias, norm_c,
                                  c_all, n_all, m_all, d, s_prev)
    y_s = y_s[:, :DEC_SEQ].reshape(ROWS_S, W_CV).astype(BF16)
    y = _fill_tail_rows(y, [y_s], [y_e])
    return y, (*p_state, *s_state)


def kernel(x_prompt, x_sample, state_rglru_conv, state_rglru_h, state_hgrn_S, state_mlstm_C, state_mlstm_n, state_mlstm_m, meta_tokens, ffn1_norm, ffn1_w_gate, ffn1_w_up, ffn1_w_down, mix_norm, ffn2_norm, ffn2_w_gate, ffn2_w_up, ffn2_w_down, even_w_in, rglru_conv_w, rglru_conv_b, rglru_w_r, rglru_b_r, rglru_w_i, rglru_b_i, rglru_lambda, hgrn_lb_raw, hgrn_norm, even_w_out, odd_w_in, mlstm_b_gates, mlstm_norm, odd_w_out, final_norm):
    meta = jnp.broadcast_to(meta_tokens[None], (BATCH, N_META, D_MODEL)).reshape(ROWS_META, D_MODEL)
    x = jnp.concatenate([x_prompt.reshape(ROWS_P, D_MODEL), x_sample.reshape(ROWS_S, D_MODEL), meta,
                         jnp.zeros((ROWS_PAD, D_MODEL), F32)], axis=0)
    lb_all = _hgrn_lower_bounds(hgrn_lb_raw)
    m_all = state_mlstm_m.reshape(N_ODD, DEC_BATCH, 1, H_C)
    odd_w_in_t = jnp.swapaxes(odd_w_in, 1, 2)
    p_conv, s_conv = [], []
    even_states = odd_states = None

    for l in range(DEPTH):
        x = _ffn_half(x, ffn1_norm, ffn1_w_gate, ffn1_w_up, ffn1_w_down, l)
        u = _rmsnorm(x, mix_norm[l], BF16)
        if l % 2 == 0:
            e = l // 2
            proj = _matmul(u, even_w_in, e, IN_EVEN)
            rg_weights = _rglru_weights(rglru_conv_w[e], rglru_conv_b[e], rglru_w_r[e], rglru_b_r[e], rglru_w_i[e],
                                        rglru_b_i[e], rglru_lambda[e])
            y, conv_p, conv_s, even_states = _mix_even(proj, e, state_rglru_conv, state_rglru_h, state_hgrn_S,
                                                       rg_weights, lb_all[e], hgrn_norm[e], even_states)
            p_conv.append(conv_p)
            s_conv.append(conv_s)
            x = _matmul_residual(y, even_w_out, e, x, 1.0)
        else:
            d = l // 2
            proj = _matmul_t(u, odd_w_in_t, d, IN_ODD_MAIN)
            gates = _matmul_tail(u, odd_w_in_t, d, IN_ODD_MAIN)
            bias = jnp.pad(mlstm_b_gates[d].reshape(1, 2 * H_C), ((0, 0), (0, LANES - 2 * H_C)))
            y, odd_states = _mix_odd(proj, gates, bias, mlstm_norm[d], d, state_mlstm_C, state_mlstm_n, m_all,
                                     odd_states)
            x = _matmul_residual(y, odd_w_out, d, x, 1.0)
        x = _ffn_half(x, ffn2_norm, ffn2_w_gate, ffn2_w_up, ffn2_w_down, l)

    y_prompt = _rmsnorm(x, final_norm, F32, TR_OUT, 0, ROWS_P).reshape(BATCH, SEQ, D_MODEL)
    y_sample = _rmsnorm(x, final_norm, F32, TR_OUT, ROW_S0, ROWS_S).reshape(DEC_BATCH, DEC_SEQ, D_MODEL)
    p_h, p_s, s_h, s_s = even_states
    p_c, p_n, p_m, s_c, s_n, s_m = odd_states
    return (y_prompt, y_sample,
            jnp.stack(p_conv), p_h.reshape(N_EVEN, BATCH, W_A), p_s, p_c, p_n, p_m.reshape(N_ODD, BATCH, H_C),
            jnp.stack(s_conv), s_h, s_s, s_c, s_n, s_m.reshape(N_ODD, DEC_BATCH, H_C))
```

```python
import functools

import jax
import jax.numpy as jnp
from jax import lax
from jax.experimental import pallas as pl
from jax.experimental.pallas import tpu as pltpu

F32 = jnp.float32
BF16 = jnp.bfloat16

D_MODEL = 4096
BATCH = 4
SEQ = 2048
DEPTH = 4
DEC_BATCH = 128
DEC_SEQ = 4
N_META = 16
N_EVEN = (DEPTH + 1) // 2
N_ODD = DEPTH // 2
D_FF = 2 * D_MODEL
EPS = 1e-6
W_A = D_MODEL // 2
H_A = 16
BW_A = W_A // H_A
CONV_W = 4
RG_C = 8.0
W_B = D_MODEL // 2
DK_B = 128
H_B = W_B // DK_B
DV_B = W_B // H_B
F_B = H_B * DK_B
HGRN_CHUNK = 32
H_C = 8
DV_C = D_MODEL // H_C
DK_C = DV_C // 2
W_CV = H_C * DV_C
W_CK = H_C * DK_C
MLSTM_CHUNK = 128
IN_EVEN = 2 * W_A + 2 * F_B + 2 * W_B
IN_ODD_MAIN = 2 * W_CK + 2 * W_CV

ROWS_P = BATCH * SEQ
ROWS_S = DEC_BATCH * DEC_SEQ
ROWS_META = BATCH * N_META
ROW_S0 = ROWS_P
ROW_META0 = ROWS_P + ROWS_S
ROWS_USED = ROW_META0 + ROWS_META
ROWS = 8832
ROWS_PAD = ROWS - ROWS_USED

LANES = 128
SUB = 8
VMEM_LIMIT = 60 * 1024 * 1024

TM = 1472
TN = 256
TN_WIDE = 512
TR_NORM = 384
TR_OUT = 512


def _cparams(*sem):
    return pltpu.CompilerParams(dimension_semantics=sem, vmem_limit_bytes=VMEM_LIMIT)


def _rmsnorm_kernel(x_ref, g_ref, o_ref):
    x = x_ref[...]
    y = x * lax.rsqrt(jnp.mean(x * x, axis=-1, keepdims=True) + EPS)
    o_ref[...] = (y * g_ref[...]).astype(o_ref.dtype)


def _rmsnorm(x, g, out_dtype, tile=TR_NORM, row0=0, rows=ROWS):
    d = x.shape[1]
    first = row0 // tile
    return pl.pallas_call(
        _rmsnorm_kernel,
        grid=(rows // tile,),
        in_specs=[pl.BlockSpec((tile, d), lambda i: (first + i, 0)),
                  pl.BlockSpec((1, d), lambda i: (0, 0))],
        out_specs=pl.BlockSpec((tile, d), lambda i: (i, 0)),
        out_shape=jax.ShapeDtypeStruct((rows, d), out_dtype),
        compiler_params=_cparams("parallel"),
        name="rmsnorm",
    )(x, g.reshape(1, d))


def _mm_kernel(a_ref, w_ref, o_ref):
    o_ref[...] = jnp.dot(a_ref[...], w_ref[...].astype(BF16), preferred_element_type=F32)


def _matmul(a, w3, layer, n_cols):
    rows, k = a.shape
    return pl.pallas_call(
        _mm_kernel,
        grid=(rows // TM, n_cols // TN_WIDE),
        in_specs=[pl.BlockSpec((TM, k), lambda m, n: (m, 0)),
                  pl.BlockSpec((None, k, TN_WIDE), lambda m, n: (layer, 0, n))],
        out_specs=pl.BlockSpec((TM, TN_WIDE), lambda m, n: (m, n)),
        out_shape=jax.ShapeDtypeStruct((rows, n_cols), F32),
        compiler_params=_cparams("parallel", "arbitrary"),
        name="matmul",
    )(a, w3)


_NT = (((1,), (1,)), ((), ()))
_TN = (((0,), (0,)), ((), ()))


def _mm_t_kernel(a_ref, wt_ref, o_ref):
    o_ref[...] = lax.dot_general(a_ref[...], wt_ref[...].astype(BF16), _NT, preferred_element_type=F32)


def _matmul_t(a, w3t, layer, n_cols):
    rows, k = a.shape
    return pl.pallas_call(
        _mm_t_kernel,
        grid=(rows // TM, n_cols // TN_WIDE),
        in_specs=[pl.BlockSpec((TM, k), lambda m, n: (m, 0)),
                  pl.BlockSpec((None, TN_WIDE, k), lambda m, n: (layer, n, 0))],
        out_specs=pl.BlockSpec((TM, TN_WIDE), lambda m, n: (m, n)),
        out_shape=jax.ShapeDtypeStruct((rows, n_cols), F32),
        compiler_params=_cparams("parallel", "arbitrary"),
        name="matmul_t",
    )(a, w3t)


def _mm_tail_kernel(a_ref, wt_ref, o_ref, *, n_valid):
    row = lax.broadcasted_iota(jnp.int32, wt_ref.shape, 0)
    wt = jnp.where(row < n_valid, wt_ref[...], 0.0).astype(BF16)
    o_ref[...] = lax.dot_general(a_ref[...], wt, _NT, preferred_element_type=F32)


def _matmul_tail(a, w3t, layer, row0):
    rows, k = a.shape
    n_valid = w3t.shape[1] - row0
    return pl.pallas_call(
        functools.partial(_mm_tail_kernel, n_valid=n_valid),
        grid=(rows // TM,),
        in_specs=[pl.BlockSpec((TM, k), lambda m: (m, 0)),
                  pl.BlockSpec((None, LANES, k), lambda m: (layer, row0 // LANES, 0))],
        out_specs=pl.BlockSpec((TM, LANES), lambda m: (m, 0)),
        out_shape=jax.ShapeDtypeStruct((rows, LANES), F32),
        compiler_params=_cparams("parallel"),
        name="matmul_tail",
    )(a, w3t)


def _gateup_kernel(a_ref, wg_ref, wu_ref, o_ref):
    a = a_ref[...]
    g = jnp.dot(a, wg_ref[...].astype(BF16), preferred_element_type=F32)
    u = jnp.dot(a, wu_ref[...].astype(BF16), preferred_element_type=F32)
    o_ref[...] = (jax.nn.silu(g) * u).astype(o_ref.dtype)


def _gateup(a, wg, wu, layer):
    rows, k = a.shape
    n = wg.shape[-1]
    wspec = pl.BlockSpec((None, k, TN), lambda m, j: (layer, 0, j))
    return pl.pallas_call(
        _gateup_kernel,
        grid=(rows // TM, n // TN),
        in_specs=[pl.BlockSpec((TM, k), lambda m, j: (m, 0)), wspec, wspec],
        out_specs=pl.BlockSpec((TM, TN), lambda m, j: (m, j)),
        out_shape=jax.ShapeDtypeStruct((rows, n), BF16),
        compiler_params=_cparams("parallel", "arbitrary"),
        name="gateup",
    )(a, wg, wu)


def _mm_res_kernel(a_ref, w_ref, x_ref, o_ref, *, scale):
    acc = jnp.dot(a_ref[...], w_ref[...].astype(BF16), preferred_element_type=F32)
    o_ref[...] = x_ref[...] + scale * acc


def _matmul_residual(a, w3, layer, x, scale):
    rows, k = a.shape
    n = w3.shape[-1]
    wide = k > D_MODEL
    a_mode = dict(pipeline_mode=pl.Buffered(1)) if wide else {}
    tn = TN if wide else TN_WIDE
    return pl.pallas_call(
        functools.partial(_mm_res_kernel, scale=scale),
        grid=(rows // TM, n // tn),
        in_specs=[pl.BlockSpec((TM, k), lambda m, j: (m, 0), **a_mode),
                  pl.BlockSpec((None, k, tn), lambda m, j: (layer, 0, j)),
                  pl.BlockSpec((TM, tn), lambda m, j: (m, j))],
        out_specs=pl.BlockSpec((TM, tn), lambda m, j: (m, j)),
        out_shape=jax.ShapeDtypeStruct((rows, n), F32),
        input_output_aliases={2: 0},
        compiler_params=_cparams("parallel", "arbitrary"),
        name="matmul_residual",
    )(a, w3, x)


def _ffn_half(x, norm_g, wg, wu, wd, layer):
    xn = _rmsnorm(x, norm_g[layer], BF16)
    h = _gateup(xn, wg, wu, layer)
    return _matmul_residual(h, wd, layer, x, 0.5)


NEG = -1e30
LOG2E = 1.4426950408889634


def _dot(a, b, dims=None):
    if dims is None:
        return jnp.dot(a, b, preferred_element_type=F32)
    return lax.dot_general(a, b, dims, preferred_element_type=F32)


def _expm1_nonpos(x):
    u = jnp.exp(x)
    um1 = u - 1.0
    y = um1 * x / jnp.where(u == 1.0, 1.0, jnp.log(u))
    y = jnp.where(u == 1.0, x, y)
    return jnp.where(um1 == -1.0, -1.0, y)


def _rms_scale(x):
    return x * lax.rsqrt(jnp.mean(x * x, axis=-1, keepdims=True) + EPS)


def _passthrough(body, n_in, n_extra):
    def wrapped(*refs):
        return body(*refs[:n_in], *refs[n_in + n_extra:])
    return wrapped


_ANY = pl.BlockSpec(memory_space=pl.ANY)


def _cumsum_rows(x):
    rows = lax.broadcasted_iota(jnp.int32, x.shape, 0)
    d = 1
    while d < x.shape[0]:
        x = x + jnp.where(rows >= d, pltpu.roll(x, d, 0), 0.0)
        d *= 2
    return x


CG = 512
HPG = CG // BW_A
RG_ROWS = N_META + SEQ
RG_PAD = SUB
RG_CHUNK = 256


def _rglru_terms(xc, wr_ref, br_ref, wi_ref, bi_ref, lam_ref):
    xcb = xc.astype(BF16)
    rs, gs = [], []
    for h in range(HPG):
        xh = xcb[:, h * BW_A:(h + 1) * BW_A]
        rs.append(_dot(xh, wr_ref[h].astype(BF16)))
        gs.append(_dot(xh, wi_ref[h].astype(BF16)))
    r = jax.nn.sigmoid(jnp.concatenate(rs, axis=1) + br_ref[...])
    i = jax.nn.sigmoid(jnp.concatenate(gs, axis=1) + bi_ref[...])
    log_a = -RG_C * r * jax.nn.softplus(-lam_ref[...])
    a = jnp.exp(log_a)
    mult = jnp.sqrt(-_expm1_nonpos(2.0 * log_a))
    return a, mult, i


def _conv4(cw_ref, cb_ref, taps):
    y = cb_ref[...]
    for j in range(CONV_W):
        y = y + cw_ref[j:j + 1, :] * taps[j]
    return y


def _rglru_prompt_kernel(xm_ref, xe_ref, gm_ref, ge_ref, cw_ref, cb_ref, wr_ref, br_ref, wi_ref, bi_ref, lam_ref,
                         ym_ref, ye_ref, hl_ref, xs_scr, a_scr, b_scr):
    xs_scr[0:RG_PAD, :] = jnp.zeros((RG_PAD, CG), F32)
    xs_scr[RG_PAD:RG_PAD + N_META, :] = xe_ref[...]
    xs_scr[RG_PAD + N_META:, :] = xm_ref[...]
    chunks = [(0, N_META)] + [(N_META + j * RG_CHUNK, RG_CHUNK) for j in range(SEQ // RG_CHUNK)]
    for t0, n in chunks:
        taps = [xs_scr[RG_PAD + t0 - d:RG_PAD + t0 - d + n, :] for d in (3, 2, 1, 0)]
        xc = _conv4(cw_ref, cb_ref, taps)
        a, mult, i = _rglru_terms(xc, wr_ref, br_ref, wi_ref, bi_ref, lam_ref)
        if t0 == 0:
            mult = jnp.where(lax.broadcasted_iota(jnp.int32, (n, CG), 0) == 0, 1.0, mult)
        a_scr[t0:t0 + n, :] = a
        b_scr[t0:t0 + n, :] = mult * i * xc

    rows8 = lax.broadcasted_iota(jnp.int32, (SUB, CG), 0)

    def scan_block(blk, h_prev):
        r0 = pl.multiple_of(blk * SUB, SUB)
        a = a_scr[pl.ds(r0, SUB), :]
        b = b_scr[pl.ds(r0, SUB), :]
        for d in (1, 2, 4):
            keep = rows8 >= d
            b = jnp.where(keep, a * pltpu.roll(b, d, 0) + b, b)
            a = jnp.where(keep, a * pltpu.roll(a, d, 0), a)
        h = b + a * h_prev
        b_scr[pl.ds(r0, SUB), :] = h
        return h[SUB - 1:SUB, :]

    hl_ref[...] = lax.fori_loop(0, RG_ROWS // SUB, scan_block, jnp.zeros((1, CG), F32))
    ye_ref[...] = (b_scr[0:N_META, :] * jax.nn.gelu(ge_ref[...])).astype(BF16)
    for j in range(SEQ // RG_CHUNK):
        rows = slice(j * RG_CHUNK, (j + 1) * RG_CHUNK)
        h = b_scr[N_META + j * RG_CHUNK:N_META + (j + 1) * RG_CHUNK, :]
        ym_ref[rows, :] = (h * jax.nn.gelu(gm_ref[rows, :])).astype(BF16)


def _rglru_weight_specs(index):
    vec = pl.BlockSpec((1, CG), lambda *g: (0, index(*g)))
    gate = pl.BlockSpec((HPG, BW_A, BW_A), lambda *g: (index(*g), 0, 0))
    return [pl.BlockSpec((CONV_W, CG), lambda *g: (0, index(*g))), vec, gate, vec, gate, vec, vec]


def _rglru_weights(cw, cb, wr, br, wi, bi, lam):
    row = lambda a: a.reshape(1, W_A)
    return (cw, row(cb), wr, row(br), wi, row(bi), row(lam))


def _rglru_prompt(proj, weights, e, h_prev):
    nm = ROW_META0 // N_META
    ng = W_A // CG
    n_in = 11
    extra = [] if h_prev is None else [h_prev]
    return pl.pallas_call(
        _passthrough(_rglru_prompt_kernel, n_in, len(extra)),
        grid=(BATCH, ng),
        in_specs=[pl.BlockSpec((SEQ, CG), lambda b, c: (b, c)),
                  pl.BlockSpec((N_META, CG), lambda b, c: (nm + b, c)),
                  pl.BlockSpec((SEQ, CG), lambda b, c: (b, ng + c)),
                  pl.BlockSpec((N_META, CG), lambda b, c: (nm + b, ng + c))]
                 + _rglru_weight_specs(lambda b, c: c) + [_ANY] * len(extra),
        out_specs=[pl.BlockSpec((SEQ, CG), lambda b, c: (b, c)),
                   pl.BlockSpec((N_META, CG), lambda b, c: (b, c)),
                   pl.BlockSpec((None, None, 1, CG), lambda b, c: (e, b, 0, c))],
        out_shape=[jax.ShapeDtypeStruct((ROWS, D_MODEL), BF16),
                   jax.ShapeDtypeStruct((ROWS_META, W_A), BF16),
                   jax.ShapeDtypeStruct((N_EVEN, BATCH, 1, W_A), F32)],
        scratch_shapes=[pltpu.VMEM((RG_PAD + RG_ROWS, CG), F32), pltpu.VMEM((RG_ROWS, CG), F32),
                        pltpu.VMEM((RG_ROWS, CG), F32)],
        input_output_aliases={n_in + i: 2 + i for i in range(len(extra))},
        compiler_params=_cparams("parallel", "parallel"),
        name="rglru_prompt",
    )(proj, proj, proj, proj, *weights, *extra)


def _rglru_sample_kernel(x_ref, g_ref, buf_ref, h0_ref, cw_ref, cb_ref, wr_ref, br_ref, wi_ref, bi_ref, lam_ref,
                         y_ref, hn_ref):
    xp = [buf_ref[j] for j in range(CONV_W - 1)] + [x_ref[t] for t in range(DEC_SEQ)]
    h = h0_ref[...]
    for t in range(DEC_SEQ):
        xc = _conv4(cw_ref, cb_ref, xp[t:t + CONV_W])
        a, mult, i = _rglru_terms(xc, wr_ref, br_ref, wi_ref, bi_ref, lam_ref)
        h = a * h + mult * i * xc
        y_ref[t] = (h * jax.nn.gelu(g_ref[t])).astype(BF16)
    hn_ref[...] = h


def _rglru_sample(xg_tm, buf_tm, h_all, weights, e, h_prev):
    ng = W_A // CG
    n_in = 11
    extra = [] if h_prev is None else [h_prev]
    state = pl.BlockSpec((None, DEC_BATCH, CG), lambda c: (e, 0, c))
    return pl.pallas_call(
        _passthrough(_rglru_sample_kernel, n_in, len(extra)),
        grid=(ng,),
        in_specs=[pl.BlockSpec((DEC_SEQ, DEC_BATCH, CG), lambda c: (0, 0, c)),
                  pl.BlockSpec((DEC_SEQ, DEC_BATCH, CG), lambda c: (0, 0, ng + c)),
                  pl.BlockSpec((CONV_W - 1, DEC_BATCH, CG), lambda c: (0, 0, c)),
                  state]
                 + _rglru_weight_specs(lambda c: c) + [_ANY] * len(extra),
        out_specs=[pl.BlockSpec((DEC_SEQ, DEC_BATCH, CG), lambda c: (0, 0, c)), state],
        out_shape=[jax.ShapeDtypeStruct((DEC_SEQ, DEC_BATCH, W_A), BF16),
                   jax.ShapeDtypeStruct((N_EVEN, DEC_BATCH, W_A), F32)],
        input_output_aliases={n_in + i: 1 + i for i in range(len(extra))},
        compiler_params=_cparams("parallel"),
        name="rglru_sample",
    )(xg_tm, xg_tm, buf_tm, h_all, *weights, *extra)


HG_ROWS = 128
DEC_PAD = SUB


def _lb_kernel(raw_ref, o_ref):
    raw = raw_ref[...]
    e = jnp.exp(raw - jnp.max(raw, axis=0, keepdims=True))
    p = e / jnp.sum(e, axis=0, keepdims=True)
    acc = jnp.zeros((1, F_B), F32)
    for r in range(N_EVEN):
        acc = acc + p[r:r + 1, :]
        o_ref[r:r + 1, :] = acc - p[0:1, :]


def _hgrn_lower_bounds(raw):
    return pl.pallas_call(_lb_kernel, out_shape=jax.ShapeDtypeStruct((N_EVEN, F_B), F32),
                          name="hgrn_lower_bounds")(raw)


def _hgrn_chunk(qb, fb, v, lb, st, n_valid):
    c = qb.shape[0]
    rows = lax.broadcasted_iota(jnp.int32, (c, DK_B), 0)
    f = lb + (1.0 - lb) * jax.nn.sigmoid(fb)
    kk = (1.0 - lb) * jax.nn.sigmoid(-fb)
    q = jax.nn.silu(qb)
    cum = _cumsum_rows(jnp.log(f))
    last = cum[n_valid - 1:n_valid, :]
    o_inter = _dot((q * jnp.exp(cum)).astype(BF16), st.astype(BF16), _NT)
    kdec = kk * jnp.exp(last - cum)
    if n_valid < c:
        kdec = jnp.where(rows < n_valid, kdec, 0.0)
    st_new = st * jnp.exp(last) + _dot(v.astype(BF16), kdec.astype(BF16), _TN)

    rows8 = lax.broadcasted_iota(jnp.int32, (SUB, DK_B), 0)
    cum2 = cum * LOG2E
    src2 = cum2 - jnp.log2(kk)
    o = o_inter
    for s in range(n_valid):
        t0 = (s // SUB) * SUB
        d = cum2[t0:c, :] - src2[s:s + 1, :]
        head = jnp.where(rows8 >= s - t0, d[0:SUB, :], NEG)
        d = head if t0 + SUB == c else jnp.concatenate([head, d[SUB:, :]], axis=0)
        p = jnp.exp2(d) * q[t0:c, :]
        term = jnp.sum(p, axis=1, keepdims=True) * v[s:s + 1, :]
        o = term + o if t0 == 0 else jnp.concatenate([o[0:t0, :], term + o[t0:c, :]], axis=0)
    return o, st_new


def _hgrn_finish(o, nb_ref, g_ref, y_ref):
    y_ref[...] = (_rms_scale(o) * nb_ref[...] * jax.nn.silu(g_ref[...])).astype(y_ref.dtype)


def _hgrn_prompt_kernel(q_ref, f_ref, i_ref, g_ref, qe_ref, fe_ref, ie_ref, ge_ref, lb_ref, nb_ref,
                        ym_ref, ye_ref, s_ref, st_scr, o_scr):
    j = pl.program_id(1)

    def run_rows(qr, fr, ir, n_rows, chunk):
        def head(h, carry):
            lanes = pl.ds(pl.multiple_of(h * DK_B, DK_B), DK_B)
            lb = lb_ref[:, lanes]
            st = st_scr[h]
            for c0 in range(0, n_rows, chunk):
                rs = slice(c0, c0 + chunk)
                o, st = _hgrn_chunk(qr[rs, lanes], fr[rs, lanes], ir[rs, lanes], lb, st, chunk)
                o_scr[rs, lanes] = o
            st_scr[h] = st
            return carry
        lax.fori_loop(0, H_B, head, 0, unroll=2)

    @pl.when(j == 0)
    def _():
        st_scr[...] = jnp.zeros(st_scr.shape, F32)
        run_rows(qe_ref, fe_ref, ie_ref, N_META, N_META)
        _hgrn_finish(o_scr[0:N_META, :], nb_ref, ge_ref, ye_ref)

    run_rows(q_ref, f_ref, i_ref, HG_ROWS, HGRN_CHUNK)
    _hgrn_finish(o_scr[...], nb_ref, g_ref, ym_ref)

    @pl.when(j == pl.num_programs(1) - 1)
    def _():
        def head(h, carry):
            s_ref[h] = st_scr[h].T
            return carry
        lax.fori_loop(0, H_B, head, 0)


def _hgrn_prompt(proj, y, lb, norm_b, e, s_prev):
    nm = ROW_META0 // N_META
    nj = SEQ // HG_ROWS
    main = lambda col: pl.BlockSpec((HG_ROWS, W_B), lambda b, j: (b * nj + j, col))
    meta = lambda col: pl.BlockSpec((N_META, W_B), lambda b, j: (nm + b, col))
    vec = pl.BlockSpec((1, W_B), lambda b, j: (0, 0))
    n_in = 10
    extra = [y] + ([] if s_prev is None else [s_prev])
    aliases = {n_in: 0}
    if s_prev is not None:
        aliases[n_in + 1] = 2
    return pl.pallas_call(
        _passthrough(_hgrn_prompt_kernel, n_in, len(extra)),
        grid=(BATCH, nj),
        in_specs=[main(2), main(3), main(4), main(5), meta(2), meta(3), meta(4), meta(5), vec, vec]
                 + [_ANY] * len(extra),
        out_specs=[pl.BlockSpec((HG_ROWS, W_B), lambda b, j: (b * nj + j, W_A // W_B)),
                   pl.BlockSpec((N_META, W_B), lambda b, j: (b, 0)),
                   pl.BlockSpec((None, None, H_B, DK_B, DV_B), lambda b, j: (e, b, 0, 0, 0))],
        out_shape=[jax.ShapeDtypeStruct((ROWS, D_MODEL), BF16),
                   jax.ShapeDtypeStruct((ROWS_META, W_B), BF16),
                   jax.ShapeDtypeStruct((N_EVEN, BATCH, H_B, DK_B, DV_B), F32)],
        scratch_shapes=[pltpu.VMEM((H_B, DV_B, DK_B), F32), pltpu.VMEM((HG_ROWS, W_B), F32)],
        input_output_aliases=aliases,
        compiler_params=_cparams("parallel", "arbitrary"),
        name="hgrn_prompt",
    )(*([proj] * 8), lb.reshape(1, F_B), norm_b.reshape(1, W_B), *extra)


HG_UNROLL = 4


def _hgrn_sample_kernel(q_ref, f_ref, i_ref, g_ref, lb_ref, nb_ref, s_in_ref, y_ref, s_out_ref, o_scr):
    def head(h, carry):
        lanes = pl.ds(pl.multiple_of(h * DK_B, DK_B), DK_B)
        o, st = _hgrn_chunk(q_ref[:, lanes], f_ref[:, lanes], i_ref[:, lanes], lb_ref[:, lanes], s_in_ref[h].T,
                            DEC_SEQ)
        o_scr[:, lanes] = o
        s_out_ref[h] = st.T
        return carry
    lax.fori_loop(0, H_B, head, 0, unroll=HG_UNROLL)
    _hgrn_finish(o_scr[...], nb_ref, g_ref, y_ref)


def _hgrn_sample(proj_s, lb, norm_b, s_all, e, s_prev):
    part = lambda col: pl.BlockSpec((None, DEC_PAD, W_B), lambda b: (b, 0, col))
    vec = pl.BlockSpec((1, W_B), lambda b: (0, 0))
    state = pl.BlockSpec((None, None, H_B, DK_B, DV_B), lambda b: (e, b, 0, 0, 0))
    n_in = 7
    extra = [] if s_prev is None else [s_prev]
    return pl.pallas_call(
        _passthrough(_hgrn_sample_kernel, n_in, len(extra)),
        grid=(DEC_BATCH,),
        in_specs=[part(2), part(3), part(4), part(5), vec, vec, state] + [_ANY] * len(extra),
        out_specs=[pl.BlockSpec((None, DEC_PAD, W_B), lambda b: (b, 0, 0)), state],
        out_shape=[jax.ShapeDtypeStruct((DEC_BATCH, DEC_PAD, W_B), F32),
                   jax.ShapeDtypeStruct((N_EVEN, DEC_BATCH, H_B, DK_B, DV_B), F32)],
        scratch_shapes=[pltpu.VMEM((DEC_PAD, W_B), F32)],
        input_output_aliases={n_in + i: 1 + i for i in range(len(extra))},
        compiler_params=_cparams("parallel"),
        name="hgrn_sample",
    )(*([proj_s] * 4), lb.reshape(1, F_B), norm_b.reshape(1, W_B), s_all, *extra)


MLSTM_GROUP_PROMPT = 2


def _mlstm_gates(ig, fg, m_prev, n_valid):
    c = ig.shape[0]
    ti = lax.broadcasted_iota(jnp.int32, (c, c), 0)
    si = lax.broadcasted_iota(jnp.int32, (c, c), 1)
    tri = si <= ti
    to_row = lambda col: jnp.sum(jnp.where(ti == si, col, 0.0), axis=0, keepdims=True)
    cum = jnp.sum(jnp.where(tri, to_row(jax.nn.log_sigmoid(fg)), 0.0), axis=1, keepdims=True)
    logw = jnp.where(tri, cum - to_row(cum) + to_row(ig), NEG)
    log_inter = cum + m_prev
    m_t = jnp.maximum(log_inter, jnp.max(logw, axis=1, keepdims=True))
    w = jnp.exp(logw - m_t)
    g = jnp.exp(log_inter - m_t)
    lv = n_valid - 1
    m_new = m_t[lv:lv + 1, :]
    cum_last = cum[lv:lv + 1, :]
    ws = jnp.exp(cum_last - cum + ig - m_new)
    if n_valid < c:
        ws = jnp.where(lax.broadcasted_iota(jnp.int32, (c, 1), 0) < n_valid, ws, 0.0)
    decay = jnp.exp(cum_last + m_prev - m_new)
    return w, g, m_t, ws, decay, m_new


def _mlstm_heads(q_ref, k_ref, v_ref, o_ref, g_ref, bias_ref, nc_ref, y_ref, get_state, put_state, n_valid, group):
    for h0 in range(0, H_C, group):
        heads = range(h0, h0 + group)
        kcols = {h: slice(h * DK_C, (h + 1) * DK_C) for h in heads}
        vcols = {h: slice(h * DV_C, (h + 1) * DV_C) for h in heads}
        state = {h: get_state(h) for h in heads}
        gates = {}
        for h in heads:
            ig = g_ref[:, h:h + 1] + bias_ref[0:1, h:h + 1]
            fg = g_ref[:, H_C + h:H_C + h + 1] + bias_ref[0:1, H_C + h:H_C + h + 1]
            gates[h] = _mlstm_gates(ig, fg, state[h][2], n_valid)
        qs = {h: q_ref[:, kcols[h]] * (DK_C ** -0.5) for h in heads}
        qb = {h: qs[h].astype(BF16) for h in heads}
        vb = {h: v_ref[:, vcols[h]].astype(BF16) for h in heads}
        qk = {h: _dot(qb[h], k_ref[:, kcols[h]].astype(BF16), _NT) for h in heads}
        inter = {h: _dot(qb[h], state[h][0].astype(BF16)) for h in heads}
        p = {h: qk[h] * gates[h][0] for h in heads}
        intra = {h: _dot(p[h].astype(BF16), vb[h]) for h in heads}
        for h in heads:
            w, g, m_t, ws, decay, m_new = gates[h]
            num = g * inter[h] + intra[h]
            den = g * jnp.sum(qs[h] * state[h][1], axis=1, keepdims=True) + jnp.sum(p[h], axis=1, keepdims=True)
            hh = num / jnp.maximum(jnp.abs(den), jnp.exp(-m_t))
            hn = _rms_scale(hh) * nc_ref[:, vcols[h]]
            y_ref[:, vcols[h]] = (jax.nn.sigmoid(o_ref[:, vcols[h]]) * hn).astype(y_ref.dtype)
        kw = {h: k_ref[:, kcols[h]] * gates[h][3] for h in heads}
        upd = {h: _dot(kw[h].astype(BF16), vb[h], _TN) for h in heads}
        for h in heads:
            C, n_row, _ = state[h]
            decay, m_new = gates[h][4], gates[h][5]
            put_state(h, decay * C + upd[h], decay * n_row + jnp.sum(kw[h], axis=0, keepdims=True), m_new)


def _mlstm_prompt_kernel(q_ref, k_ref, v_ref, o_ref, g_ref, qe_ref, ke_ref, ve_ref, oe_ref, ge_ref, bias_ref, nc_ref,
                         ym_ref, ye_ref, c_ref, n_ref, m_ref):
    def get_state(h):
        return c_ref[h], n_ref[h:h + 1, :], m_ref[0:1, h:h + 1]

    def put_state(h, C, n_row, m_new):
        c_ref[h] = C
        n_ref[h:h + 1, :] = n_row
        m_ref[0:1, h:h + 1] = m_new

    @pl.when(pl.program_id(1) == 0)
    def _():
        c_ref[...] = jnp.zeros(c_ref.shape, F32)
        n_ref[...] = jnp.zeros(n_ref.shape, F32)
        m_ref[...] = jnp.zeros(m_ref.shape, F32)
        _mlstm_heads(qe_ref, ke_ref, ve_ref, oe_ref, ge_ref, bias_ref, nc_ref, ye_ref, get_state, put_state, N_META,
                     MLSTM_GROUP_PROMPT)

    _mlstm_heads(q_ref, k_ref, v_ref, o_ref, g_ref, bias_ref, nc_ref, ym_ref, get_state, put_state, MLSTM_CHUNK,
                 MLSTM_GROUP_PROMPT)


def _mlstm_state_specs(d, index):
    return [pl.BlockSpec((None, None, H_C, DK_C, DV_C), lambda *g: (d, index(*g), 0, 0, 0)),
            pl.BlockSpec((None, None, H_C, DK_C), lambda *g: (d, index(*g), 0, 0)),
            pl.BlockSpec((None, None, 1, H_C), lambda *g: (d, index(*g), 0, 0))]


def _mlstm_state_shapes(batch):
    return [jax.ShapeDtypeStruct((N_ODD, batch, H_C, DK_C, DV_C), F32),
            jax.ShapeDtypeStruct((N_ODD, batch, H_C, DK_C), F32),
            jax.ShapeDtypeStruct((N_ODD, batch, 1, H_C), F32)]


def _mlstm_prompt(proj, gates, bias, norm_c, d, prev):
    nm = ROW_META0 // N_META
    nj = SEQ // MLSTM_CHUNK
    main = lambda width, col: pl.BlockSpec((MLSTM_CHUNK, width), lambda b, j: (b * nj + j, col))
    meta = lambda width, col: pl.BlockSpec((N_META, width), lambda b, j: (nm + b, col))
    n_in = 12
    extra = [] if prev is None else list(prev)
    return pl.pallas_call(
        _passthrough(_mlstm_prompt_kernel, n_in, len(extra)),
        grid=(BATCH, nj),
        in_specs=[main(W_CK, 0), main(W_CK, 1), main(W_CV, 1), main(W_CV, 2), main(LANES, 0),
                  meta(W_CK, 0), meta(W_CK, 1), meta(W_CV, 1), meta(W_CV, 2), meta(LANES, 0),
                  pl.BlockSpec((1, LANES), lambda b, j: (0, 0)), pl.BlockSpec((1, W_CV), lambda b, j: (0, 0))]
                 + [_ANY] * len(extra),
        out_specs=[pl.BlockSpec((MLSTM_CHUNK, W_CV), lambda b, j: (b * nj + j, 0)),
                   pl.BlockSpec((N_META, W_CV), lambda b, j: (b, 0))]
                  + _mlstm_state_specs(d, lambda b, j: b),
        out_shape=[jax.ShapeDtypeStruct((ROWS, W_CV), BF16),
                   jax.ShapeDtypeStruct((ROWS_META, W_CV), BF16)] + _mlstm_state_shapes(BATCH),
        input_output_aliases={n_in + i: 2 + i for i in range(len(extra))},
        compiler_params=_cparams("parallel", "arbitrary"),
        name="mlstm_prompt",
    )(proj, proj, proj, proj, gates, proj, proj, proj, proj, gates, bias, norm_c.reshape(1, W_CV), *extra)


def _mlstm_sample_kernel(q_ref, k_ref, v_ref, o_ref, g_ref, bias_ref, nc_ref, c0_ref, n0_ref, m0_ref,
                         y_ref, c_ref, n_ref, m_ref):
    def get_state(h):
        return c0_ref[h], n0_ref[h:h + 1, :], m0_ref[0:1, h:h + 1]

    def put_state(h, C, n_row, m_new):
        c_ref[h] = C
        n_ref[h:h + 1, :] = n_row
        m_ref[0:1, h:h + 1] = m_new

    _mlstm_heads(q_ref, k_ref, v_ref, o_ref, g_ref, bias_ref, nc_ref, y_ref, get_state, put_state, DEC_SEQ, H_C)


def _mlstm_sample(proj_s, gates_s, bias, norm_c, c_all, n_all, m_all, d, prev):
    part = lambda width, col: pl.BlockSpec((None, DEC_PAD, width), lambda b: (b, 0, col))
    states = _mlstm_state_specs(d, lambda b: b)
    n_in = 10
    extra = [] if prev is None else list(prev)
    return pl.pallas_call(
        _passthrough(_mlstm_sample_kernel, n_in, len(extra)),
        grid=(DEC_BATCH,),
        in_specs=[part(W_CK, 0), part(W_CK, 1), part(W_CV, 1), part(W_CV, 2), part(LANES, 0),
                  pl.BlockSpec((1, LANES), lambda b: (0, 0)), pl.BlockSpec((1, W_CV), lambda b: (0, 0))]
                 + states + [_ANY] * len(extra),
        out_specs=[pl.BlockSpec((None, DEC_PAD, W_CV), lambda b: (b, 0, 0))] + states,
        out_shape=[jax.ShapeDtypeStruct((DEC_BATCH, DEC_PAD, W_CV), F32)] + _mlstm_state_shapes(DEC_BATCH),
        input_output_aliases={n_in + i: 1 + i for i in range(len(extra))},
        compiler_params=_cparams("parallel"),
        name="mlstm_sample",
    )(proj_s, proj_s, proj_s, proj_s, gates_s, bias, norm_c.reshape(1, W_CV), c_all, n_all, m_all, *extra)


def _sample_rows(a):
    return a[ROW_S0:ROW_META0].reshape(DEC_BATCH, DEC_SEQ, a.shape[-1])


def _pad_steps(a):
    return jnp.pad(a, ((0, 0), (0, DEC_PAD - DEC_SEQ), (0, 0)))


def _fill_tail_rows(y, sample, meta):
    cat = lambda parts: parts[0] if len(parts) == 1 else jnp.concatenate(parts, axis=1)
    tail = jnp.concatenate([cat(sample), cat(meta), jnp.zeros((ROWS_PAD, y.shape[1]), y.dtype)], axis=0)
    return lax.dynamic_update_slice(y, tail, (ROWS_P, 0))


def _mix_even(proj, e, conv_all, h_all, s_all, rg_weights, lb, norm_b, prev):
    p_h0, p_s0, s_h0, s_s0 = prev if prev is not None else (None,) * 4
    y, ya_e, p_h = _rglru_prompt(proj, rg_weights, e, p_h0)
    y, yb_e, p_s = _hgrn_prompt(proj, y, lb, norm_b, e, p_s0)
    ps = _sample_rows(proj)
    xg_tm = jnp.swapaxes(ps[:, :, :2 * W_A], 0, 1)
    ya_s, s_h = _rglru_sample(xg_tm, jnp.swapaxes(conv_all[e], 0, 1), h_all, rg_weights, e, s_h0)
    ya_s = jnp.swapaxes(ya_s, 0, 1).reshape(ROWS_S, W_A)
    yb_s, s_s = _hgrn_sample(_pad_steps(ps), lb, norm_b, s_all, e, s_s0)
    yb_s = yb_s[:, :DEC_SEQ].reshape(ROWS_S, W_B).astype(BF16)
    y = _fill_tail_rows(y, [ya_s, yb_s], [ya_e, yb_e])
    conv_p = jnp.stack([proj[(b + 1) * SEQ - (CONV_W - 1):(b + 1) * SEQ, :W_A] for b in range(BATCH)])
    conv_s = ps[:, DEC_SEQ - (CONV_W - 1):, :W_A]
    return y, conv_p, conv_s, (p_h, p_s, s_h, s_s)


def _mix_odd(proj, gates, bias, norm_c, d, c_all, n_all, m_all, prev):
    p_prev, s_prev = (prev[:3], prev[3:]) if prev is not None else (None, None)
    y, y_e, *p_state = _mlstm_prompt(proj, gates, bias, norm_c, d, p_prev)
    y_s, *s_state = _mlstm_sample(_pad_steps(_sample_rows(proj)), _pad_steps(_sample_rows(gates)), bias, norm_c,
                                  c_all, n_all, m_all, d, s_prev)
    y_s = y_s[:, :DEC_SEQ].reshape(ROWS_S, W_CV).astype(BF16)
    y = _fill_tail_rows(y, [y_s], [y_e])
    return y, (*p_state, *s_state)


def kernel(x_prompt, x_sample, state_rglru_conv, state_rglru_h, state_hgrn_S, state_mlstm_C, state_mlstm_n, state_mlstm_m, meta_tokens, ffn1_norm, ffn1_w_gate, ffn1_w_up, ffn1_w_down, mix_norm, ffn2_norm, ffn2_w_gate, ffn2_w_up, ffn2_w_down, even_w_in, rglru_conv_w, rglru_conv_b, rglru_w_r, rglru_b_r, rglru_w_i, rglru_b_i, rglru_lambda, hgrn_lb_raw, hgrn_norm, even_w_out, odd_w_in, mlstm_b_gates, mlstm_norm, odd_w_out, final_norm):
    meta = jnp.broadcast_to(meta_tokens[None], (BATCH, N_META, D_MODEL)).reshape(ROWS_META, D_MODEL)
    x = jnp.concatenate([x_prompt.reshape(ROWS_P, D_MODEL), x_sample.reshape(ROWS_S, D_MODEL), meta,
                         jnp.zeros((ROWS_PAD, D_MODEL), F32)], axis=0)
    lb_all = _hgrn_lower_bounds(hgrn_lb_raw)
    m_all = state_mlstm_m.reshape(N_ODD, DEC_BATCH, 1, H_C)
    odd_w_in_t = jnp.swapaxes(odd_w_in, 1, 2)
    p_conv, s_conv = [], []
    even_states = odd_states = None

    for l in range(DEPTH):
        x = _ffn_half(x, ffn1_norm, ffn1_w_gate, ffn1_w_up, ffn1_w_down, l)
        u = _rmsnorm(x, mix_norm[l], BF16)
        if l % 2 == 0:
            e = l // 2
            proj = _matmul(u, even_w_in, e, IN_EVEN)
            rg_weights = _rglru_weights(rglru_conv_w[e], rglru_conv_b[e], rglru_w_r[e], rglru_b_r[e], rglru_w_i[e],
                                        rglru_b_i[e], rglru_lambda[e])
            y, conv_p, conv_s, even_states = _mix_even(proj, e, state_rglru_conv, state_rglru_h, state_hgrn_S,
                                                       rg_weights, lb_all[e], hgrn_norm[e], even_states)
            p_conv.append(conv_p)
            s_conv.append(conv_s)
            x = _matmul_residual(y, even_w_out, e, x, 1.0)
        else:
            d = l // 2
            proj = _matmul_t(u, odd_w_in_t, d, IN_ODD_MAIN)
            gates = _matmul_tail(u, odd_w_in_t, d, IN_ODD_MAIN)
            bias = jnp.pad(mlstm_b_gates[d].reshape(1, 2 * H_C), ((0, 0), (0, LANES - 2 * H_C)))
            y, odd_states = _mix_odd(proj, gates, bias, mlstm_norm[d], d, state_mlstm_C, state_mlstm_n, m_all,
                                     odd_states)
            x = _matmul_residual(y, odd_w_out, d, x, 1.0)
        x = _ffn_half(x, ffn2_norm, ffn2_w_gate, ffn2_w_up, ffn2_w_down, l)

    y_prompt = _rmsnorm(x, final_norm, F32, TR_OUT, 0, ROWS_P).reshape(BATCH, SEQ, D_MODEL)
    y_sample = _rmsnorm(x, final_norm, F32, TR_OUT, ROW_S0, ROWS_S).reshape(DEC_BATCH, DEC_SEQ, D_MODEL)
    p_h, p_s, s_h, s_s = even_states
    p_c, p_n, p_m, s_c, s_n, s_m = odd_states
    return (y_prompt, y_sample,
            jnp.stack(p_conv), p_h.reshape(N_EVEN, BATCH, W_A), p_s, p_c, p_n, p_m.reshape(N_ODD, BATCH, H_C),
            jnp.stack(s_conv), s_h, s_s, s_c, s_n, s_m.reshape(N_ODD, DEC_BATCH, H_C))
```

```python
import functools

import jax
import jax.numpy as jnp
from jax import lax
from jax.experimental import pallas as pl
from jax.experimental.pallas import tpu as pltpu

F32 = jnp.float32
BF16 = jnp.bfloat16

D_MODEL = 4096
BATCH = 4
SEQ = 2048
DEPTH = 4
DEC_BATCH = 128
DEC_SEQ = 4
N_META = 16
N_EVEN = (DEPTH + 1) // 2
N_ODD = DEPTH // 2
D_FF = 2 * D_MODEL
EPS = 1e-6
W_A = D_MODEL // 2
H_A = 16
BW_A = W_A // H_A
CONV_W = 4
RG_C = 8.0
W_B = D_MODEL // 2
DK_B = 128
H_B = W_B // DK_B
DV_B = W_B // H_B
F_B = H_B * DK_B
HGRN_CHUNK = 32
H_C = 8
DV_C = D_MODEL // H_C
DK_C = DV_C // 2
W_CV = H_C * DV_C
W_CK = H_C * DK_C
MLSTM_CHUNK = 128
IN_EVEN = 2 * W_A + 2 * F_B + 2 * W_B
IN_ODD_MAIN = 2 * W_CK + 2 * W_CV

ROWS_P = BATCH * SEQ
ROWS_S = DEC_BATCH * DEC_SEQ
ROWS_META = BATCH * N_META
ROW_S0 = ROWS_P
ROW_META0 = ROWS_P + ROWS_S
ROWS_USED = ROW_META0 + ROWS_META
ROWS = 8832
ROWS_PAD = ROWS - ROWS_USED

LANES = 128
SUB = 8
VMEM_LIMIT = 60 * 1024 * 1024

TM = 1472
TM_TALL = 2944
TN = 256
TN_WIDE = 512
TR_NORM = 736
TR_OUT = 512


def _cparams(*sem):
    return pltpu.CompilerParams(dimension_semantics=sem, vmem_limit_bytes=VMEM_LIMIT)


def _rmsnorm_kernel(x_ref, g_ref, o_ref):
    x = x_ref[...]
    y = x * lax.rsqrt(jnp.mean(x * x, axis=-1, keepdims=True) + EPS)
    o_ref[...] = (y * g_ref[...]).astype(o_ref.dtype)


def _rmsnorm(x, g, out_dtype, tile=TR_NORM, row0=0, rows=ROWS):
    d = x.shape[1]
    first = row0 // tile
    return pl.pallas_call(
        _rmsnorm_kernel,
        grid=(rows // tile,),
        in_specs=[pl.BlockSpec((tile, d), lambda i: (first + i, 0)),
                  pl.BlockSpec((1, d), lambda i: (0, 0))],
        out_specs=pl.BlockSpec((tile, d), lambda i: (i, 0)),
        out_shape=jax.ShapeDtypeStruct((rows, d), out_dtype),
        compiler_params=_cparams("parallel"),
        name="rmsnorm",
    )(x, g.reshape(1, d))


def _mm_kernel(a_ref, w_ref, o_ref):
    o_ref[...] = jnp.dot(a_ref[...], w_ref[...].astype(BF16), preferred_element_type=F32)


def _matmul(a, w3, layer, n_cols):
    rows, k = a.shape
    return pl.pallas_call(
        _mm_kernel,
        grid=(rows // TM, n_cols // TN_WIDE),
        in_specs=[pl.BlockSpec((TM, k), lambda m, n: (m, 0)),
                  pl.BlockSpec((None, k, TN_WIDE), lambda m, n: (layer, 0, n))],
        out_specs=pl.BlockSpec((TM, TN_WIDE), lambda m, n: (m, n)),
        out_shape=jax.ShapeDtypeStruct((rows, n_cols), F32),
        compiler_params=_cparams("parallel", "arbitrary"),
        name="matmul",
    )(a, w3)


_NT = (((1,), (1,)), ((), ()))
_TN = (((0,), (0,)), ((), ()))


def _mm_t_kernel(a_ref, wt_ref, o_ref):
    o_ref[...] = lax.dot_general(a_ref[...], wt_ref[...].astype(BF16), _NT, preferred_element_type=F32)


def _matmul_t(a, w3t, layer, n_cols):
    rows, k = a.shape
    return pl.pallas_call(
        _mm_t_kernel,
        grid=(rows // TM, n_cols // TN_WIDE),
        in_specs=[pl.BlockSpec((TM, k), lambda m, n: (m, 0)),
                  pl.BlockSpec((None, TN_WIDE, k), lambda m, n: (layer, n, 0))],
        out_specs=pl.BlockSpec((TM, TN_WIDE), lambda m, n: (m, n)),
        out_shape=jax.ShapeDtypeStruct((rows, n_cols), F32),
        compiler_params=_cparams("parallel", "arbitrary"),
        name="matmul_t",
    )(a, w3t)


def _mm_tail_kernel(a_ref, wt_ref, o_ref, *, n_valid):
    row = lax.broadcasted_iota(jnp.int32, wt_ref.shape, 0)
    wt = jnp.where(row < n_valid, wt_ref[...], 0.0).astype(BF16)
    o_ref[...] = lax.dot_general(a_ref[...], wt, _NT, preferred_element_type=F32)


def _matmul_tail(a, w3t, layer, row0):
    rows, k = a.shape
    n_valid = w3t.shape[1] - row0
    return pl.pallas_call(
        functools.partial(_mm_tail_kernel, n_valid=n_valid),
        grid=(rows // TM,),
        in_specs=[pl.BlockSpec((TM, k), lambda m: (m, 0)),
                  pl.BlockSpec((None, LANES, k), lambda m: (layer, row0 // LANES, 0))],
        out_specs=pl.BlockSpec((TM, LANES), lambda m: (m, 0)),
        out_shape=jax.ShapeDtypeStruct((rows, LANES), F32),
        compiler_params=_cparams("parallel"),
        name="matmul_tail",
    )(a, w3t)


def _gateup_kernel(a_ref, wg_ref, wu_ref, o_ref):
    a = a_ref[...]
    g = jnp.dot(a, wg_ref[...].astype(BF16), preferred_element_type=F32)
    u = jnp.dot(a, wu_ref[...].astype(BF16), preferred_element_type=F32)
    o_ref[...] = (jax.nn.silu(g) * u).astype(o_ref.dtype)


def _gateup(a, wg, wu, layer):
    rows, k = a.shape
    n = wg.shape[-1]
    wspec = pl.BlockSpec((None, k, TN), lambda m, j: (layer, 0, j))
    a_spec = pl.BlockSpec((TM_TALL, k), lambda m, j: (m, 0), pipeline_mode=pl.Buffered(1))
    return pl.pallas_call(
        _gateup_kernel,
        grid=(rows // TM_TALL, n // TN),
        in_specs=[a_spec, wspec, wspec],
        out_specs=pl.BlockSpec((TM_TALL, TN), lambda m, j: (m, j)),
        out_shape=jax.ShapeDtypeStruct((rows, n), BF16),
        compiler_params=_cparams("parallel", "arbitrary"),
        name="gateup",
    )(a, wg, wu)


def _mm_res_kernel(a_ref, w_ref, x_ref, o_ref, *, scale):
    acc = jnp.dot(a_ref[...], w_ref[...].astype(BF16), preferred_element_type=F32)
    o_ref[...] = x_ref[...] + scale * acc


def _matmul_residual(a, w3, layer, x, scale):
    rows, k = a.shape
    n = w3.shape[-1]
    wide = k > D_MODEL
    a_mode = dict(pipeline_mode=pl.Buffered(1)) if wide else {}
    tn = TN if wide else TN_WIDE
    return pl.pallas_call(
        functools.partial(_mm_res_kernel, scale=scale),
        grid=(rows // TM, n // tn),
        in_specs=[pl.BlockSpec((TM, k), lambda m, j: (m, 0), **a_mode),
                  pl.BlockSpec((None, k, tn), lambda m, j: (layer, 0, j)),
                  pl.BlockSpec((TM, tn), lambda m, j: (m, j))],
        out_specs=pl.BlockSpec((TM, tn), lambda m, j: (m, j)),
        out_shape=jax.ShapeDtypeStruct((rows, n), F32),
        input_output_aliases={2: 0},
        compiler_params=_cparams("parallel", "arbitrary"),
        name="matmul_residual",
    )(a, w3, x)


def _ffn_half(x, norm_g, wg, wu, wd, layer):
    xn = _rmsnorm(x, norm_g[layer], BF16)
    h = _gateup(xn, wg, wu, layer)
    return _matmul_residual(h, wd, layer, x, 0.5)


NEG = -1e30
LOG2E = 1.4426950408889634


def _dot(a, b, dims=None):
    if dims is None:
        return jnp.dot(a, b, preferred_element_type=F32)
    return lax.dot_general(a, b, dims, preferred_element_type=F32)


def _expm1_nonpos(x):
    u = jnp.exp(x)
    um1 = u - 1.0
    y = um1 * x / jnp.where(u == 1.0, 1.0, jnp.log(u))
    y = jnp.where(u == 1.0, x, y)
    return jnp.where(um1 == -1.0, -1.0, y)


def _rms_scale(x):
    return x * lax.rsqrt(jnp.mean(x * x, axis=-1, keepdims=True) + EPS)


def _passthrough(body, n_in, n_extra):
    def wrapped(*refs):
        return body(*refs[:n_in], *refs[n_in + n_extra:])
    return wrapped


_ANY = pl.BlockSpec(memory_space=pl.ANY)


def _cumsum_rows(x):
    rows = lax.broadcasted_iota(jnp.int32, x.shape, 0)
    d = 1
    while d < x.shape[0]:
        x = x + jnp.where(rows >= d, pltpu.roll(x, d, 0), 0.0)
        d *= 2
    return x


CG = 512
HPG = CG // BW_A
RG_ROWS = N_META + SEQ
RG_PAD = SUB
RG_CHUNK = 256


def _rglru_terms(xc, wr_ref, br_ref, wi_ref, bi_ref, lam_ref):
    xcb = xc.astype(BF16)
    rs, gs = [], []
    for h in range(HPG):
        xh = xcb[:, h * BW_A:(h + 1) * BW_A]
        rs.append(_dot(xh, wr_ref[h].astype(BF16)))
        gs.append(_dot(xh, wi_ref[h].astype(BF16)))
    r = jax.nn.sigmoid(jnp.concatenate(rs, axis=1) + br_ref[...])
    i = jax.nn.sigmoid(jnp.concatenate(gs, axis=1) + bi_ref[...])
    log_a = -RG_C * r * jax.nn.softplus(-lam_ref[...])
    a = jnp.exp(log_a)
    mult = jnp.sqrt(-_expm1_nonpos(2.0 * log_a))
    return a, mult, i


def _conv4(cw_ref, cb_ref, taps):
    y = cb_ref[...]
    for j in range(CONV_W):
        y = y + cw_ref[j:j + 1, :] * taps[j]
    return y


def _rglru_prompt_kernel(xm_ref, xe_ref, gm_ref, ge_ref, cw_ref, cb_ref, wr_ref, br_ref, wi_ref, bi_ref, lam_ref,
                         ym_ref, ye_ref, hl_ref, xs_scr, a_scr, b_scr):
    xs_scr[0:RG_PAD, :] = jnp.zeros((RG_PAD, CG), F32)
    xs_scr[RG_PAD:RG_PAD + N_META, :] = xe_ref[...]
    xs_scr[RG_PAD + N_META:, :] = xm_ref[...]
    chunks = [(0, N_META)] + [(N_META + j * RG_CHUNK, RG_CHUNK) for j in range(SEQ // RG_CHUNK)]
    for t0, n in chunks:
        taps = [xs_scr[RG_PAD + t0 - d:RG_PAD + t0 - d + n, :] for d in (3, 2, 1, 0)]
        xc = _conv4(cw_ref, cb_ref, taps)
        a, mult, i = _rglru_terms(xc, wr_ref, br_ref, wi_ref, bi_ref, lam_ref)
        if t0 == 0:
            mult = jnp.where(lax.broadcasted_iota(jnp.int32, (n, CG), 0) == 0, 1.0, mult)
        a_scr[t0:t0 + n, :] = a
        b_scr[t0:t0 + n, :] = mult * i * xc

    rows8 = lax.broadcasted_iota(jnp.int32, (SUB, CG), 0)

    def scan_block(blk, h_prev):
        r0 = pl.multiple_of(blk * SUB, SUB)
        a = a_scr[pl.ds(r0, SUB), :]
        b = b_scr[pl.ds(r0, SUB), :]
        for d in (1, 2, 4):
            keep = rows8 >= d
            b = jnp.where(keep, a * pltpu.roll(b, d, 0) + b, b)
            a = jnp.where(keep, a * pltpu.roll(a, d, 0), a)
        h = b + a * h_prev
        b_scr[pl.ds(r0, SUB), :] = h
        return h[SUB - 1:SUB, :]

    hl_ref[...] = lax.fori_loop(0, RG_ROWS // SUB, scan_block, jnp.zeros((1, CG), F32))
    ye_ref[...] = (b_scr[0:N_META, :] * jax.nn.gelu(ge_ref[...])).astype(BF16)
    for j in range(SEQ // RG_CHUNK):
        rows = slice(j * RG_CHUNK, (j + 1) * RG_CHUNK)
        h = b_scr[N_META + j * RG_CHUNK:N_META + (j + 1) * RG_CHUNK, :]
        ym_ref[rows, :] = (h * jax.nn.gelu(gm_ref[rows, :])).astype(BF16)


def _rglru_weight_specs(index):
    vec = pl.BlockSpec((1, CG), lambda *g: (0, index(*g)))
    gate = pl.BlockSpec((HPG, BW_A, BW_A), lambda *g: (index(*g), 0, 0))
    return [pl.BlockSpec((CONV_W, CG), lambda *g: (0, index(*g))), vec, gate, vec, gate, vec, vec]


def _rglru_weights(cw, cb, wr, br, wi, bi, lam):
    row = lambda a: a.reshape(1, W_A)
    return (cw, row(cb), wr, row(br), wi, row(bi), row(lam))


def _rglru_prompt(proj, weights, e, h_prev):
    nm = ROW_META0 // N_META
    ng = W_A // CG
    n_in = 11
    extra = [] if h_prev is None else [h_prev]
    return pl.pallas_call(
        _passthrough(_rglru_prompt_kernel, n_in, len(extra)),
        grid=(BATCH, ng),
        in_specs=[pl.BlockSpec((SEQ, CG), lambda b, c: (b, c)),
                  pl.BlockSpec((N_META, CG), lambda b, c: (nm + b, c)),
                  pl.BlockSpec((SEQ, CG), lambda b, c: (b, ng + c)),
                  pl.BlockSpec((N_META, CG), lambda b, c: (nm + b, ng + c))]
                 + _rglru_weight_specs(lambda b, c: c) + [_ANY] * len(extra),
        out_specs=[pl.BlockSpec((SEQ, CG), lambda b, c: (b, c)),
                   pl.BlockSpec((N_META, CG), lambda b, c: (b, c)),
                   pl.BlockSpec((None, None, 1, CG), lambda b, c: (e, b, 0, c))],
        out_shape=[jax.ShapeDtypeStruct((ROWS, D_MODEL), BF16),
                   jax.ShapeDtypeStruct((ROWS_META, W_A), BF16),
                   jax.ShapeDtypeStruct((N_EVEN, BATCH, 1, W_A), F32)],
        scratch_shapes=[pltpu.VMEM((RG_PAD + RG_ROWS, CG), F32), pltpu.VMEM((RG_ROWS, CG), F32),
                        pltpu.VMEM((RG_ROWS, CG), F32)],
        input_output_aliases={n_in + i: 2 + i for i in range(len(extra))},
        compiler_params=_cparams("parallel", "parallel"),
        name="rglru_prompt",
    )(proj, proj, proj, proj, *weights, *extra)


def _rglru_sample_kernel(x_ref, g_ref, buf_ref, h0_ref, cw_ref, cb_ref, wr_ref, br_ref, wi_ref, bi_ref, lam_ref,
                         y_ref, hn_ref):
    xp = [buf_ref[j] for j in range(CONV_W - 1)] + [x_ref[t] for t in range(DEC_SEQ)]
    h = h0_ref[...]
    for t in range(DEC_SEQ):
        xc = _conv4(cw_ref, cb_ref, xp[t:t + CONV_W])
        a, mult, i = _rglru_terms(xc, wr_ref, br_ref, wi_ref, bi_ref, lam_ref)
        h = a * h + mult * i * xc
        y_ref[t] = (h * jax.nn.gelu(g_ref[t])).astype(BF16)
    hn_ref[...] = h


def _rglru_sample(xg_tm, buf_tm, h_all, weights, e, h_prev):
    ng = W_A // CG
    n_in = 11
    extra = [] if h_prev is None else [h_prev]
    state = pl.BlockSpec((None, DEC_BATCH, CG), lambda c: (e, 0, c))
    return pl.pallas_call(
        _passthrough(_rglru_sample_kernel, n_in, len(extra)),
        grid=(ng,),
        in_specs=[pl.BlockSpec((DEC_SEQ, DEC_BATCH, CG), lambda c: (0, 0, c)),
                  pl.BlockSpec((DEC_SEQ, DEC_BATCH, CG), lambda c: (0, 0, ng + c)),
                  pl.BlockSpec((CONV_W - 1, DEC_BATCH, CG), lambda c: (0, 0, c)),
                  state]
                 + _rglru_weight_specs(lambda c: c) + [_ANY] * len(extra),
        out_specs=[pl.BlockSpec((DEC_SEQ, DEC_BATCH, CG), lambda c: (0, 0, c)), state],
        out_shape=[jax.ShapeDtypeStruct((DEC_SEQ, DEC_BATCH, W_A), BF16),
                   jax.ShapeDtypeStruct((N_EVEN, DEC_BATCH, W_A), F32)],
        input_output_aliases={n_in + i: 1 + i for i in range(len(extra))},
        compiler_params=_cparams("parallel"),
        name="rglru_sample",
    )(xg_tm, xg_tm, buf_tm, h_all, *weights, *extra)


HG_ROWS = 128
DEC_PAD = SUB


def _lb_kernel(raw_ref, o_ref):
    raw = raw_ref[...]
    e = jnp.exp(raw - jnp.max(raw, axis=0, keepdims=True))
    p = e / jnp.sum(e, axis=0, keepdims=True)
    acc = jnp.zeros((1, F_B), F32)
    for r in range(N_EVEN):
        acc = acc + p[r:r + 1, :]
        o_ref[r:r + 1, :] = acc - p[0:1, :]


def _hgrn_lower_bounds(raw):
    return pl.pallas_call(_lb_kernel, out_shape=jax.ShapeDtypeStruct((N_EVEN, F_B), F32),
                          name="hgrn_lower_bounds")(raw)


def _hgrn_chunk(qb, fb, v, lb, st, n_valid):
    c = qb.shape[0]
    rows = lax.broadcasted_iota(jnp.int32, (c, DK_B), 0)
    f = lb + (1.0 - lb) * jax.nn.sigmoid(fb)
    kk = (1.0 - lb) * jax.nn.sigmoid(-fb)
    q = jax.nn.silu(qb)
    cum = _cumsum_rows(jnp.log(f))
    last = cum[n_valid - 1:n_valid, :]
    o_inter = _dot((q * jnp.exp(cum)).astype(BF16), st.astype(BF16), _NT)
    kdec = kk * jnp.exp(last - cum)
    if n_valid < c:
        kdec = jnp.where(rows < n_valid, kdec, 0.0)
    st_new = st * jnp.exp(last) + _dot(v.astype(BF16), kdec.astype(BF16), _TN)

    rows8 = lax.broadcasted_iota(jnp.int32, (SUB, DK_B), 0)
    cum2 = cum * LOG2E
    src2 = cum2 - jnp.log2(kk)
    o = o_inter
    for s in range(n_valid):
        t0 = (s // SUB) * SUB
        d = cum2[t0:c, :] - src2[s:s + 1, :]
        head = jnp.where(rows8 >= s - t0, d[0:SUB, :], NEG)
        d = head if t0 + SUB == c else jnp.concatenate([head, d[SUB:, :]], axis=0)
        p = jnp.exp2(d) * q[t0:c, :]
        term = jnp.sum(p, axis=1, keepdims=True) * v[s:s + 1, :]
        o = term + o if t0 == 0 else jnp.concatenate([o[0:t0, :], term + o[t0:c, :]], axis=0)
    return o, st_new


def _hgrn_finish(o, nb_ref, g_ref, y_ref):
    y_ref[...] = (_rms_scale(o) * nb_ref[...] * jax.nn.silu(g_ref[...])).astype(y_ref.dtype)


def _hgrn_prompt_kernel(q_ref, f_ref, i_ref, g_ref, qe_ref, fe_ref, ie_ref, ge_ref, lb_ref, nb_ref,
                        ym_ref, ye_ref, s_ref, st_scr, o_scr):
    j = pl.program_id(1)

    def run_rows(qr, fr, ir, n_rows, chunk):
        def head(h, carry):
            lanes = pl.ds(pl.multiple_of(h * DK_B, DK_B), DK_B)
            lb = lb_ref[:, lanes]
            st = st_scr[h]
            for c0 in range(0, n_rows, chunk):
                rs = slice(c0, c0 + chunk)
                o, st = _hgrn_chunk(qr[rs, lanes], fr[rs, lanes], ir[rs, lanes], lb, st, chunk)
                o_scr[rs, lanes] = o
            st_scr[h] = st
            return carry
        lax.fori_loop(0, H_B, head, 0, unroll=2)

    @pl.when(j == 0)
    def _():
        st_scr[...] = jnp.zeros(st_scr.shape, F32)
        run_rows(qe_ref, fe_ref, ie_ref, N_META, N_META)
        _hgrn_finish(o_scr[0:N_META, :], nb_ref, ge_ref, ye_ref)

    run_rows(q_ref, f_ref, i_ref, HG_ROWS, HGRN_CHUNK)
    _hgrn_finish(o_scr[...], nb_ref, g_ref, ym_ref)

    @pl.when(j == pl.num_programs(1) - 1)
    def _():
        def head(h, carry):
            s_ref[h] = st_scr[h].T
            return carry
        lax.fori_loop(0, H_B, head, 0)


def _hgrn_prompt(proj, y, lb, norm_b, e, s_prev):
    nm = ROW_META0 // N_META
    nj = SEQ // HG_ROWS
    main = lambda col: pl.BlockSpec((HG_ROWS, W_B), lambda b, j: (b * nj + j, col))
    meta = lambda col: pl.BlockSpec((N_META, W_B), lambda b, j: (nm + b, col))
    vec = pl.BlockSpec((1, W_B), lambda b, j: (0, 0))
    n_in = 10
    extra = [y] + ([] if s_prev is None else [s_prev])
    aliases = {n_in: 0}
    if s_prev is not None:
        aliases[n_in + 1] = 2
    return pl.pallas_call(
        _passthrough(_hgrn_prompt_kernel, n_in, len(extra)),
        grid=(BATCH, nj),
        in_specs=[main(2), main(3), main(4), main(5), meta(2), meta(3), meta(4), meta(5), vec, vec]
                 + [_ANY] * len(extra),
        out_specs=[pl.BlockSpec((HG_ROWS, W_B), lambda b, j: (b * nj + j, W_A // W_B)),
                   pl.BlockSpec((N_META, W_B), lambda b, j: (b, 0)),
                   pl.BlockSpec((None, None, H_B, DK_B, DV_B), lambda b, j: (e, b, 0, 0, 0))],
        out_shape=[jax.ShapeDtypeStruct((ROWS, D_MODEL), BF16),
                   jax.ShapeDtypeStruct((ROWS_META, W_B), BF16),
                   jax.ShapeDtypeStruct((N_EVEN, BATCH, H_B, DK_B, DV_B), F32)],
        scratch_shapes=[pltpu.VMEM((H_B, DV_B, DK_B), F32), pltpu.VMEM((HG_ROWS, W_B), F32)],
        input_output_aliases=aliases,
        compiler_params=_cparams("parallel", "arbitrary"),
        name="hgrn_prompt",
    )(*([proj] * 8), lb.reshape(1, F_B), norm_b.reshape(1, W_B), *extra)


HG_UNROLL = 4


def _hgrn_sample_kernel(q_ref, f_ref, i_ref, g_ref, lb_ref, nb_ref, s_in_ref, y_ref, s_out_ref, o_scr):
    def head(h, carry):
        lanes = pl.ds(pl.multiple_of(h * DK_B, DK_B), DK_B)
        o, st = _hgrn_chunk(q_ref[:, lanes], f_ref[:, lanes], i_ref[:, lanes], lb_ref[:, lanes], s_in_ref[h].T,
                            DEC_SEQ)
        o_scr[:, lanes] = o
        s_out_ref[h] = st.T
        return carry
    lax.fori_loop(0, H_B, head, 0, unroll=HG_UNROLL)
    _hgrn_finish(o_scr[...], nb_ref, g_ref, y_ref)


def _hgrn_sample(proj_s, lb, norm_b, s_all, e, s_prev):
    part = lambda col: pl.BlockSpec((None, DEC_PAD, W_B), lambda b: (b, 0, col))
    vec = pl.BlockSpec((1, W_B), lambda b: (0, 0))
    state = pl.BlockSpec((None, None, H_B, DK_B, DV_B), lambda b: (e, b, 0, 0, 0))
    n_in = 7
    extra = [] if s_prev is None else [s_prev]
    return pl.pallas_call(
        _passthrough(_hgrn_sample_kernel, n_in, len(extra)),
        grid=(DEC_BATCH,),
        in_specs=[part(2), part(3), part(4), part(5), vec, vec, state] + [_ANY] * len(extra),
        out_specs=[pl.BlockSpec((None, DEC_PAD, W_B), lambda b: (b, 0, 0)), state],
        out_shape=[jax.ShapeDtypeStruct((DEC_BATCH, DEC_PAD, W_B), F32),
                   jax.ShapeDtypeStruct((N_EVEN, DEC_BATCH, H_B, DK_B, DV_B), F32)],
        scratch_shapes=[pltpu.VMEM((DEC_PAD, W_B), F32)],
        input_output_aliases={n_in + i: 1 + i for i in range(len(extra))},
        compiler_params=_cparams("parallel"),
        name="hgrn_sample",
    )(*([proj_s] * 4), lb.reshape(1, F_B), norm_b.reshape(1, W_B), s_all, *extra)


MLSTM_GROUP_PROMPT = 2


def _mlstm_gates(ig, fg, m_prev, n_valid):
    c = ig.shape[0]
    ti = lax.broadcasted_iota(jnp.int32, (c, c), 0)
    si = lax.broadcasted_iota(jnp.int32, (c, c), 1)
    tri = si <= ti
    to_row = lambda col: jnp.sum(jnp.where(ti == si, col, 0.0), axis=0, keepdims=True)
    cum = jnp.sum(jnp.where(tri, to_row(jax.nn.log_sigmoid(fg)), 0.0), axis=1, keepdims=True)
    logw = jnp.where(tri, cum - to_row(cum) + to_row(ig), NEG)
    log_inter = cum + m_prev
    m_t = jnp.maximum(log_inter, jnp.max(logw, axis=1, keepdims=True))
    w = jnp.exp(logw - m_t)
    g = jnp.exp(log_inter - m_t)
    lv = n_valid - 1
    m_new = m_t[lv:lv + 1, :]
    cum_last = cum[lv:lv + 1, :]
    ws = jnp.exp(cum_last - cum + ig - m_new)
    if n_valid < c:
        ws = jnp.where(lax.broadcasted_iota(jnp.int32, (c, 1), 0) < n_valid, ws, 0.0)
    decay = jnp.exp(cum_last + m_prev - m_new)
    return w, g, m_t, ws, decay, m_new


def _mlstm_heads(q_ref, k_ref, v_ref, o_ref, g_ref, bias_ref, nc_ref, y_ref, get_state, put_state, n_valid, group):
    for h0 in range(0, H_C, group):
        heads = range(h0, h0 + group)
        kcols = {h: slice(h * DK_C, (h + 1) * DK_C) for h in heads}
        vcols = {h: slice(h * DV_C, (h + 1) * DV_C) for h in heads}
        state = {h: get_state(h) for h in heads}
        gates = {}
        for h in heads:
            ig = g_ref[:, h:h + 1] + bias_ref[0:1, h:h + 1]
            fg = g_ref[:, H_C + h:H_C + h + 1] + bias_ref[0:1, H_C + h:H_C + h + 1]
            gates[h] = _mlstm_gates(ig, fg, state[h][2], n_valid)
        qs = {h: q_ref[:, kcols[h]] * (DK_C ** -0.5) for h in heads}
        qb = {h: qs[h].astype(BF16) for h in heads}
        vb = {h: v_ref[:, vcols[h]].astype(BF16) for h in heads}
        qk = {h: _dot(qb[h], k_ref[:, kcols[h]].astype(BF16), _NT) for h in heads}
        inter = {h: _dot(qb[h], state[h][0].astype(BF16)) for h in heads}
        p = {h: qk[h] * gates[h][0] for h in heads}
        intra = {h: _dot(p[h].astype(BF16), vb[h]) for h in heads}
        for h in heads:
            w, g, m_t, ws, decay, m_new = gates[h]
            num = g * inter[h] + intra[h]
            den = g * jnp.sum(qs[h] * state[h][1], axis=1, keepdims=True) + jnp.sum(p[h], axis=1, keepdims=True)
            hh = num / jnp.maximum(jnp.abs(den), jnp.exp(-m_t))
            hn = _rms_scale(hh) * nc_ref[:, vcols[h]]
            y_ref[:, vcols[h]] = (jax.nn.sigmoid(o_ref[:, vcols[h]]) * hn).astype(y_ref.dtype)
        kw = {h: k_ref[:, kcols[h]] * gates[h][3] for h in heads}
        upd = {h: _dot(kw[h].astype(BF16), vb[h], _TN) for h in heads}
        for h in heads:
            C, n_row, _ = state[h]
            decay, m_new = gates[h][4], gates[h][5]
            put_state(h, decay * C + upd[h], decay * n_row + jnp.sum(kw[h], axis=0, keepdims=True), m_new)


def _mlstm_prompt_kernel(q_ref, k_ref, v_ref, o_ref, g_ref, qe_ref, ke_ref, ve_ref, oe_ref, ge_ref, bias_ref, nc_ref,
                         ym_ref, ye_ref, c_ref, n_ref, m_ref):
    def get_state(h):
        return c_ref[h], n_ref[h:h + 1, :], m_ref[0:1, h:h + 1]

    def put_state(h, C, n_row, m_new):
        c_ref[h] = C
        n_ref[h:h + 1, :] = n_row
        m_ref[0:1, h:h + 1] = m_new

    @pl.when(pl.program_id(1) == 0)
    def _():
        c_ref[...] = jnp.zeros(c_ref.shape, F32)
        n_ref[...] = jnp.zeros(n_ref.shape, F32)
        m_ref[...] = jnp.zeros(m_ref.shape, F32)
        _mlstm_heads(qe_ref, ke_ref, ve_ref, oe_ref, ge_ref, bias_ref, nc_ref, ye_ref, get_state, put_state, N_META,
                     MLSTM_GROUP_PROMPT)

    _mlstm_heads(q_ref, k_ref, v_ref, o_ref, g_ref, bias_ref, nc_ref, ym_ref, get_state, put_state, MLSTM_CHUNK,
                 MLSTM_GROUP_PROMPT)


def _mlstm_state_specs(d, index):
    return [pl.BlockSpec((None, None, H_C, DK_C, DV_C), lambda *g: (d, index(*g), 0, 0, 0)),
            pl.BlockSpec((None, None, H_C, DK_C), lambda *g: (d, index(*g), 0, 0)),
            pl.BlockSpec((None, None, 1, H_C), lambda *g: (d, index(*g), 0, 0))]


def _mlstm_state_shapes(batch):
    return [jax.ShapeDtypeStruct((N_ODD, batch, H_C, DK_C, DV_C), F32),
            jax.ShapeDtypeStruct((N_ODD, batch, H_C, DK_C), F32),
            jax.ShapeDtypeStruct((N_ODD, batch, 1, H_C), F32)]


def _mlstm_prompt(proj, gates, bias, norm_c, d, prev):
    nm = ROW_META0 // N_META
    nj = SEQ // MLSTM_CHUNK
    main = lambda width, col: pl.BlockSpec((MLSTM_CHUNK, width), lambda b, j: (b * nj + j, col))
    meta = lambda width, col: pl.BlockSpec((N_META, width), lambda b, j: (nm + b, col))
    n_in = 12
    extra = [] if prev is None else list(prev)
    return pl.pallas_call(
        _passthrough(_mlstm_prompt_kernel, n_in, len(extra)),
        grid=(BATCH, nj),
        in_specs=[main(W_CK, 0), main(W_CK, 1), main(W_CV, 1), main(W_CV, 2), main(LANES, 0),
                  meta(W_CK, 0), meta(W_CK, 1), meta(W_CV, 1), meta(W_CV, 2), meta(LANES, 0),
                  pl.BlockSpec((1, LANES), lambda b, j: (0, 0)), pl.BlockSpec((1, W_CV), lambda b, j: (0, 0))]
                 + [_ANY] * len(extra),
        out_specs=[pl.BlockSpec((MLSTM_CHUNK, W_CV), lambda b, j: (b * nj + j, 0)),
                   pl.BlockSpec((N_META, W_CV), lambda b, j: (b, 0))]
                  + _mlstm_state_specs(d, lambda b, j: b),
        out_shape=[jax.ShapeDtypeStruct((ROWS, W_CV), BF16),
                   jax.ShapeDtypeStruct((ROWS_META, W_CV), BF16)] + _mlstm_state_shapes(BATCH),
        input_output_aliases={n_in + i: 2 + i for i in range(len(extra))},
        compiler_params=_cparams("parallel", "arbitrary"),
        name="mlstm_prompt",
    )(proj, proj, proj, proj, gates, proj, proj, proj, proj, gates, bias, norm_c.reshape(1, W_CV), *extra)


def _mlstm_sample_kernel(q_ref, k_ref, v_ref, o_ref, g_ref, bias_ref, nc_ref, c0_ref, n0_ref, m0_ref,
                         y_ref, c_ref, n_ref, m_ref):
    def get_state(h):
        return c0_ref[h], n0_ref[h:h + 1, :], m0_ref[0:1, h:h + 1]

    def put_state(h, C, n_row, m_new):
        c_ref[h] = C
        n_ref[h:h + 1, :] = n_row
        m_ref[0:1, h:h + 1] = m_new

    _mlstm_heads(q_ref, k_ref, v_ref, o_ref, g_ref, bias_ref, nc_ref, y_ref, get_state, put_state, DEC_SEQ, H_C)


def _mlstm_sample(proj_s, gates_s, bias, norm_c, c_all, n_all, m_all, d, prev):
    part = lambda width, col: pl.BlockSpec((None, DEC_PAD, width), lambda b: (b, 0, col))
    states = _mlstm_state_specs(d, lambda b: b)
    n_in = 10
    extra = [] if prev is None else list(prev)
    return pl.pallas_call(
        _passthrough(_mlstm_sample_kernel, n_in, len(extra)),
        grid=(DEC_BATCH,),
        in_specs=[part(W_CK, 0), part(W_CK, 1), part(W_CV, 1), part(W_CV, 2), part(LANES, 0),
                  pl.BlockSpec((1, LANES), lambda b: (0, 0)), pl.BlockSpec((1, W_CV), lambda b: (0, 0))]
                 + states + [_ANY] * len(extra),
        out_specs=[pl.BlockSpec((None, DEC_PAD, W_CV), lambda b: (b, 0, 0))] + states,
        out_shape=[jax.ShapeDtypeStruct((DEC_BATCH, DEC_PAD, W_CV), F32)] + _mlstm_state_shapes(DEC_BATCH),
        input_output_aliases={n_in + i: 1 + i for i in range(len(extra))},
        compiler_params=_cparams("parallel"),
        name="mlstm_sample",
    )(proj_s, proj_s, proj_s, proj_s, gates_s, bias, norm_c.reshape(1, W_CV), c_all, n_all, m_all, *extra)


def _sample_rows(a):
    return a[ROW_S0:ROW_META0].reshape(DEC_BATCH, DEC_SEQ, a.shape[-1])


def _pad_steps(a):
    return jnp.pad(a, ((0, 0), (0, DEC_PAD - DEC_SEQ), (0, 0)))


def _fill_tail_rows(y, sample, meta):
    cat = lambda parts: parts[0] if len(parts) == 1 else jnp.concatenate(parts, axis=1)
    tail = jnp.concatenate([cat(sample), cat(meta), jnp.zeros((ROWS_PAD, y.shape[1]), y.dtype)], axis=0)
    return lax.dynamic_update_slice(y, tail, (ROWS_P, 0))


def _mix_even(proj, e, conv_all, h_all, s_all, rg_weights, lb, norm_b, prev):
    p_h0, p_s0, s_h0, s_s0 = prev if prev is not None else (None,) * 4
    y, ya_e, p_h = _rglru_prompt(proj, rg_weights, e, p_h0)
    y, yb_e, p_s = _hgrn_prompt(proj, y, lb, norm_b, e, p_s0)
    ps = _sample_rows(proj)
    xg_tm = jnp.swapaxes(ps[:, :, :2 * W_A], 0, 1)
    ya_s, s_h = _rglru_sample(xg_tm, jnp.swapaxes(conv_all[e], 0, 1), h_all, rg_weights, e, s_h0)
    ya_s = jnp.swapaxes(ya_s, 0, 1).reshape(ROWS_S, W_A)
    yb_s, s_s = _hgrn_sample(_pad_steps(ps), lb, norm_b, s_all, e, s_s0)
    yb_s = yb_s[:, :DEC_SEQ].reshape(ROWS_S, W_B).astype(BF16)
    y = _fill_tail_rows(y, [ya_s, yb_s], [ya_e, yb_e])
    conv_p = jnp.stack([proj[(b + 1) * SEQ - (CONV_W - 1):(b + 1) * SEQ, :W_A] for b in range(BATCH)])
    conv_s = ps[:, DEC_SEQ - (CONV_W - 1):, :W_A]
    return y, conv_p, conv_s, (p_h, p_s, s_h, s_s)


def _mix_odd(proj, gates, bias, norm_c, d, c_all, n_all, m_all, prev):
    p_prev, s_prev = (prev[:3], prev[3:]) if prev is not None else (None, None)
    y, y_e, *p_state = _mlstm_prompt(proj, gates, bias, norm_c, d, p_prev)
    y_s, *s_state = _mlstm_sample(_pad_steps(_sample_rows(proj)), _pad_steps(_sample_rows(gates)), bias, norm_c,
                                  c_all, n_all, m_all, d, s_prev)
    y_s = y_s[:, :DEC_SEQ].reshape(ROWS_S, W_CV).astype(BF16)
    y = _fill_tail_rows(y, [y_s], [y_e])
    return y, (*p_state, *s_state)


def kernel(x_prompt, x_sample, state_rglru_conv, state_rglru_h, state_hgrn_S, state_mlstm_C, state_mlstm_n, state_mlstm_m, meta_tokens, ffn1_norm, ffn1_w_gate, ffn1_w_up, ffn1_w_down, mix_norm, ffn2_norm, ffn2_w_gate, ffn2_w_up, ffn2_w_down, even_w_in, rglru_conv_w, rglru_conv_b, rglru_w_r, rglru_b_r, rglru_w_i, rglru_b_i, rglru_lambda, hgrn_lb_raw, hgrn_norm, even_w_out, odd_w_in, mlstm_b_gates, mlstm_norm, odd_w_out, final_norm):
    meta = jnp.broadcast_to(meta_tokens[None], (BATCH, N_META, D_MODEL)).reshape(ROWS_META, D_MODEL)
    x = jnp.concatenate([x_prompt.reshape(ROWS_P, D_MODEL), x_sample.reshape(ROWS_S, D_MODEL), meta,
                         jnp.zeros((ROWS_PAD, D_MODEL), F32)], axis=0)
    lb_all = _hgrn_lower_bounds(hgrn_lb_raw)
    m_all = state_mlstm_m.reshape(N_ODD, DEC_BATCH, 1, H_C)
    odd_w_in_t = jnp.swapaxes(odd_w_in, 1, 2)
    p_conv, s_conv = [], []
    even_states = odd_states = None

    for l in range(DEPTH):
        x = _ffn_half(x, ffn1_norm, ffn1_w_gate, ffn1_w_up, ffn1_w_down, l)
        u = _rmsnorm(x, mix_norm[l], BF16)
        if l % 2 == 0:
            e = l // 2
            proj = _matmul(u, even_w_in, e, IN_EVEN)
            rg_weights = _rglru_weights(rglru_conv_w[e], rglru_conv_b[e], rglru_w_r[e], rglru_b_r[e], rglru_w_i[e],
                                        rglru_b_i[e], rglru_lambda[e])
            y, conv_p, conv_s, even_states = _mix_even(proj, e, state_rglru_conv, state_rglru_h, state_hgrn_S,
                                                       rg_weights, lb_all[e], hgrn_norm[e], even_states)
            p_conv.append(conv_p)
            s_conv.append(conv_s)
            x = _matmul_residual(y, even_w_out, e, x, 1.0)
        else:
            d = l // 2
            proj = _matmul_t(u, odd_w_in_t, d, IN_ODD_MAIN)
            gates = _matmul_tail(u, odd_w_in_t, d, IN_ODD_MAIN)
            bias = jnp.pad(mlstm_b_gates[d].reshape(1, 2 * H_C), ((0, 0), (0, LANES - 2 * H_C)))
            y, odd_states = _mix_odd(proj, gates, bias, mlstm_norm[d], d, state_mlstm_C, state_mlstm_n, m_all,
                                     odd_states)
            x = _matmul_residual(y, odd_w_out, d, x, 1.0)
        x = _ffn_half(x, ffn2_norm, ffn2_w_gate, ffn2_w_up, ffn2_w_down, l)

    y_prompt = _rmsnorm(x, final_norm, F32, TR_OUT, 0, ROWS_P).reshape(BATCH, SEQ, D_MODEL)
    y_sample = _rmsnorm(x, final_norm, F32, TR_OUT, ROW_S0, ROWS_S).reshape(DEC_BATCH, DEC_SEQ, D_MODEL)
    p_h, p_s, s_h, s_s = even_states
    p_c, p_n, p_m, s_c, s_n, s_m = odd_states
    return (y_prompt, y_sample,
            jnp.stack(p_conv), p_h.reshape(N_EVEN, BATCH, W_A), p_s, p_c, p_n, p_m.reshape(N_ODD, BATCH, H_C),
            jnp.stack(s_conv), s_h, s_s, s_c, s_n, s_m.reshape(N_ODD, DEC_BATCH, H_C))
```

```python
import functools

import jax
import jax.numpy as jnp
from jax import lax
from jax.experimental import pallas as pl
from jax.experimental.pallas import tpu as pltpu

F32 = jnp.float32
BF16 = jnp.bfloat16

D_MODEL = 4096
BATCH = 4
SEQ = 2048
DEPTH = 4
DEC_BATCH = 128
DEC_SEQ = 4
N_META = 16
N_EVEN = (DEPTH + 1) // 2
N_ODD = DEPTH // 2
D_FF = 2 * D_MODEL
EPS = 1e-6
W_A = D_MODEL // 2
H_A = 16
BW_A = W_A // H_A
CONV_W = 4
RG_C = 8.0
W_B = D_MODEL // 2
DK_B = 128
H_B = W_B // DK_B
DV_B = W_B // H_B
F_B = H_B * DK_B
HGRN_CHUNK = 32
H_C = 8
DV_C = D_MODEL // H_C
DK_C = DV_C // 2
W_CV = H_C * DV_C
W_CK = H_C * DK_C
MLSTM_CHUNK = 128
IN_EVEN = 2 * W_A + 2 * F_B + 2 * W_B
IN_ODD_MAIN = 2 * W_CK + 2 * W_CV

ROWS_P = BATCH * SEQ
ROWS_S = DEC_BATCH * DEC_SEQ
ROWS_META = BATCH * N_META
ROW_S0 = ROWS_P
ROW_META0 = ROWS_P + ROWS_S
ROWS_USED = ROW_META0 + ROWS_META
ROWS = 8832
ROWS_PAD = ROWS - ROWS_USED

LANES = 128
SUB = 8
VMEM_LIMIT = 60 * 1024 * 1024

TM = 1472
TM_TALL = 2944
TN = 256
TN_WIDE = 512
TR_NORM = 736
TR_OUT = 512


def _cparams(*sem):
    return pltpu.CompilerParams(dimension_semantics=sem, vmem_limit_bytes=VMEM_LIMIT)


def _rmsnorm_kernel(x_ref, g_ref, o_ref):
    x = x_ref[...]
    y = x * lax.rsqrt(jnp.mean(x * x, axis=-1, keepdims=True) + EPS)
    o_ref[...] = (y * g_ref[...]).astype(o_ref.dtype)


def _rmsnorm(x, g, out_dtype, tile=TR_NORM, row0=0, rows=ROWS):
    d = x.shape[1]
    first = row0 // tile
    return pl.pallas_call(
        _rmsnorm_kernel,
        grid=(rows // tile,),
        in_specs=[pl.BlockSpec((tile, d), lambda i: (first + i, 0)),
                  pl.BlockSpec((1, d), lambda i: (0, 0))],
        out_specs=pl.BlockSpec((tile, d), lambda i: (i, 0)),
        out_shape=jax.ShapeDtypeStruct((rows, d), out_dtype),
        compiler_params=_cparams("parallel"),
        name="rmsnorm",
    )(x, g.reshape(1, d))


def _mm_kernel(a_ref, w_ref, o_ref):
    o_ref[...] = jnp.dot(a_ref[...], w_ref[...].astype(BF16), preferred_element_type=F32)


def _matmul(a, w3, layer, n_cols):
    rows, k = a.shape
    return pl.pallas_call(
        _mm_kernel,
        grid=(rows // TM, n_cols // TN_WIDE),
        in_specs=[pl.BlockSpec((TM, k), lambda m, n: (m, 0)),
                  pl.BlockSpec((None, k, TN_WIDE), lambda m, n: (layer, 0, n))],
        out_specs=pl.BlockSpec((TM, TN_WIDE), lambda m, n: (m, n)),
        out_shape=jax.ShapeDtypeStruct((rows, n_cols), F32),
        compiler_params=_cparams("parallel", "arbitrary"),
        name="matmul",
    )(a, w3)


_NT = (((1,), (1,)), ((), ()))
_TN = (((0,), (0,)), ((), ()))


def _mm_t_kernel(a_ref, wt_ref, o_ref):
    o_ref[...] = lax.dot_general(a_ref[...], wt_ref[...].astype(BF16), _NT, preferred_element_type=F32)


def _matmul_t(a, w3t, layer, n_cols):
    rows, k = a.shape
    return pl.pallas_call(
        _mm_t_kernel,
        grid=(rows // TM, n_cols // TN_WIDE),
        in_specs=[pl.BlockSpec((TM, k), lambda m, n: (m, 0)),
                  pl.BlockSpec((None, TN_WIDE, k), lambda m, n: (layer, n, 0))],
        out_specs=pl.BlockSpec((TM, TN_WIDE), lambda m, n: (m, n)),
        out_shape=jax.ShapeDtypeStruct((rows, n_cols), F32),
        compiler_params=_cparams("parallel", "arbitrary"),
        name="matmul_t",
    )(a, w3t)


def _mm_tail_kernel(a_ref, wt_ref, o_ref, *, n_valid):
    row = lax.broadcasted_iota(jnp.int32, wt_ref.shape, 0)
    wt = jnp.where(row < n_valid, wt_ref[...], 0.0).astype(BF16)
    o_ref[...] = lax.dot_general(a_ref[...], wt, _NT, preferred_element_type=F32)


def _matmul_tail(a, w3t, layer, row0):
    rows, k = a.shape
    n_valid = w3t.shape[1] - row0
    return pl.pallas_call(
        functools.partial(_mm_tail_kernel, n_valid=n_valid),
        grid=(rows // TM,),
        in_specs=[pl.BlockSpec((TM, k), lambda m: (m, 0)),
                  pl.BlockSpec((None, LANES, k), lambda m: (layer, row0 // LANES, 0))],
        out_specs=pl.BlockSpec((TM, LANES), lambda m: (m, 0)),
        out_shape=jax.ShapeDtypeStruct((rows, LANES), F32),
        compiler_params=_cparams("parallel"),
        name="matmul_tail",
    )(a, w3t)


def _gateup_kernel(a_ref, wg_ref, wu_ref, o_ref):
    a = a_ref[...]
    g = jnp.dot(a, wg_ref[...].astype(BF16), preferred_element_type=F32)
    u = jnp.dot(a, wu_ref[...].astype(BF16), preferred_element_type=F32)
    o_ref[...] = (jax.nn.silu(g) * u).astype(o_ref.dtype)


def _gateup(a, wg, wu, layer):
    rows, k = a.shape
    n = wg.shape[-1]
    wspec = pl.BlockSpec((None, k, TN), lambda m, j: (layer, 0, j))
    a_spec = pl.BlockSpec((TM_TALL, k), lambda m, j: (m, 0), pipeline_mode=pl.Buffered(1))
    return pl.pallas_call(
        _gateup_kernel,
        grid=(rows // TM_TALL, n // TN),
        in_specs=[a_spec, wspec, wspec],
        out_specs=pl.BlockSpec((TM_TALL, TN), lambda m, j: (m, j)),
        out_shape=jax.ShapeDtypeStruct((rows, n), BF16),
        compiler_params=_cparams("parallel", "arbitrary"),
        name="gateup",
    )(a, wg, wu)


def _mm_res_kernel(a_ref, w_ref, x_ref, o_ref, *, scale):
    acc = jnp.dot(a_ref[...], w_ref[...].astype(BF16), preferred_element_type=F32)
    o_ref[...] = x_ref[...] + scale * acc


def _matmul_residual(a, w3, layer, x, scale):
    rows, k = a.shape
    n = w3.shape[-1]
    wide = k > D_MODEL
    a_mode = dict(pipeline_mode=pl.Buffered(1)) if wide else {}
    tn = TN if wide else TN_WIDE
    return pl.pallas_call(
        functools.partial(_mm_res_kernel, scale=scale),
        grid=(rows // TM, n // tn),
        in_specs=[pl.BlockSpec((TM, k), lambda m, j: (m, 0), **a_mode),
                  pl.BlockSpec((None, k, tn), lambda m, j: (layer, 0, j)),
                  pl.BlockSpec((TM, tn), lambda m, j: (m, j))],
        out_specs=pl.BlockSpec((TM, tn), lambda m, j: (m, j)),
        out_shape=jax.ShapeDtypeStruct((rows, n), F32),
        input_output_aliases={2: 0},
        compiler_params=_cparams("parallel", "arbitrary"),
        name="matmul_residual",
    )(a, w3, x)


def _ffn_half(x, norm_g, wg, wu, wd, layer):
    xn = _rmsnorm(x, norm_g[layer], BF16)
    h = _gateup(xn, wg, wu, layer)
    return _matmul_residual(h, wd, layer, x, 0.5)


NEG = -1e30
LOG2E = 1.4426950408889634


def _dot(a, b, dims=None):
    if dims is None:
        return jnp.dot(a, b, preferred_element_type=F32)
    return lax.dot_general(a, b, dims, preferred_element_type=F32)


def _expm1_nonpos(x):
    u = jnp.exp(x)
    um1 = u - 1.0
    y = um1 * x / jnp.where(u == 1.0, 1.0, jnp.log(u))
    y = jnp.where(u == 1.0, x, y)
    return jnp.where(um1 == -1.0, -1.0, y)


def _rms_scale(x):
    return x * lax.rsqrt(jnp.mean(x * x, axis=-1, keepdims=True) + EPS)


def _passthrough(body, n_in, n_extra):
    def wrapped(*refs):
        return body(*refs[:n_in], *refs[n_in + n_extra:])
    return wrapped


_ANY = pl.BlockSpec(memory_space=pl.ANY)


def _cumsum_rows(x):
    rows = lax.broadcasted_iota(jnp.int32, x.shape, 0)
    d = 1
    while d < x.shape[0]:
        x = x + jnp.where(rows >= d, pltpu.roll(x, d, 0), 0.0)
        d *= 2
    return x


CG = 512
HPG = CG // BW_A
RG_ROWS = N_META + SEQ
RG_PAD = SUB
RG_CHUNK = 256


def _rglru_terms(xc, wr_ref, br_ref, wi_ref, bi_ref, lam_ref):
    xcb = xc.astype(BF16)
    rs, gs = [], []
    for h in range(HPG):
        xh = xcb[:, h * BW_A:(h + 1) * BW_A]
        rs.append(_dot(xh, wr_ref[h].astype(BF16)))
        gs.append(_dot(xh, wi_ref[h].astype(BF16)))
    r = jax.nn.sigmoid(jnp.concatenate(rs, axis=1) + br_ref[...])
    i = jax.nn.sigmoid(jnp.concatenate(gs, axis=1) + bi_ref[...])
    log_a = -RG_C * r * jax.nn.softplus(-lam_ref[...])
    a = jnp.exp(log_a)
    mult = jnp.sqrt(-_expm1_nonpos(2.0 * log_a))
    return a, mult, i


def _conv4(cw_ref, cb_ref, taps):
    y = cb_ref[...]
    for j in range(CONV_W):
        y = y + cw_ref[j:j + 1, :] * taps[j]
    return y


def _rglru_prompt_kernel(xm_ref, xe_ref, gm_ref, ge_ref, cw_ref, cb_ref, wr_ref, br_ref, wi_ref, bi_ref, lam_ref,
                         ym_ref, ye_ref, hl_ref, xs_scr, a_scr, b_scr):
    xs_scr[0:RG_PAD, :] = jnp.zeros((RG_PAD, CG), F32)
    xs_scr[RG_PAD:RG_PAD + N_META, :] = xe_ref[...]
    xs_scr[RG_PAD + N_META:, :] = xm_ref[...]
    chunks = [(0, N_META)] + [(N_META + j * RG_CHUNK, RG_CHUNK) for j in range(SEQ // RG_CHUNK)]
    for t0, n in chunks:
        taps = [xs_scr[RG_PAD + t0 - d:RG_PAD + t0 - d + n, :] for d in (3, 2, 1, 0)]
        xc = _conv4(cw_ref, cb_ref, taps)
        a, mult, i = _rglru_terms(xc, wr_ref, br_ref, wi_ref, bi_ref, lam_ref)
        if t0 == 0:
            mult = jnp.where(lax.broadcasted_iota(jnp.int32, (n, CG), 0) == 0, 1.0, mult)
        a_scr[t0:t0 + n, :] = a
        b_scr[t0:t0 + n, :] = mult * i * xc

    rows8 = lax.broadcasted_iota(jnp.int32, (SUB, CG), 0)

    def scan_block(blk, h_prev):
        r0 = pl.multiple_of(blk * SUB, SUB)
        a = a_scr[pl.ds(r0, SUB), :]
        b = b_scr[pl.ds(r0, SUB), :]
        for d in (1, 2, 4):
            keep = rows8 >= d
            b = jnp.where(keep, a * pltpu.roll(b, d, 0) + b, b)
            a = jnp.where(keep, a * pltpu.roll(a, d, 0), a)
        h = b + a * h_prev
        b_scr[pl.ds(r0, SUB), :] = h
        return h[SUB - 1:SUB, :]

    hl_ref[...] = lax.fori_loop(0, RG_ROWS // SUB, scan_block, jnp.zeros((1, CG), F32))
    ye_ref[...] = (b_scr[0:N_META, :] * jax.nn.gelu(ge_ref[...])).astype(BF16)
    for j in range(SEQ // RG_CHUNK):
        rows = slice(j * RG_CHUNK, (j + 1) * RG_CHUNK)
        h = b_scr[N_META + j * RG_CHUNK:N_META + (j + 1) * RG_CHUNK, :]
        ym_ref[rows, :] = (h * jax.nn.gelu(gm_ref[rows, :])).astype(BF16)


def _rglru_weight_specs(index):
    vec = pl.BlockSpec((1, CG), lambda *g: (0, index(*g)))
    gate = pl.BlockSpec((HPG, BW_A, BW_A), lambda *g: (index(*g), 0, 0))
    return [pl.BlockSpec((CONV_W, CG), lambda *g: (0, index(*g))), vec, gate, vec, gate, vec, vec]


def _rglru_weights(cw, cb, wr, br, wi, bi, lam):
    row = lambda a: a.reshape(1, W_A)
    return (cw, row(cb), wr, row(br), wi, row(bi), row(lam))


def _rglru_prompt(proj, weights, e, h_prev):
    nm = ROW_META0 // N_META
    ng = W_A // CG
    n_in = 11
    extra = [] if h_prev is None else [h_prev]
    return pl.pallas_call(
        _passthrough(_rglru_prompt_kernel, n_in, len(extra)),
        grid=(BATCH, ng),
        in_specs=[pl.BlockSpec((SEQ, CG), lambda b, c: (b, c)),
                  pl.BlockSpec((N_META, CG), lambda b, c: (nm + b, c)),
                  pl.BlockSpec((SEQ, CG), lambda b, c: (b, ng + c)),
                  pl.BlockSpec((N_META, CG), lambda b, c: (nm + b, ng + c))]
                 + _rglru_weight_specs(lambda b, c: c) + [_ANY] * len(extra),
        out_specs=[pl.BlockSpec((SEQ, CG), lambda b, c: (b, c)),
                   pl.BlockSpec((N_META, CG), lambda b, c: (b, c)),
                   pl.BlockSpec((None, None, 1, CG), lambda b, c: (e, b, 0, c))],
        out_shape=[jax.ShapeDtypeStruct((ROWS, D_MODEL), BF16),
                   jax.ShapeDtypeStruct((ROWS_META, W_A), BF16),
                   jax.ShapeDtypeStruct((N_EVEN, BATCH, 1, W_A), F32)],
        scratch_shapes=[pltpu.VMEM((RG_PAD + RG_ROWS, CG), F32), pltpu.VMEM((RG_ROWS, CG), F32),
                        pltpu.VMEM((RG_ROWS, CG), F32)],
        input_output_aliases={n_in + i: 2 + i for i in range(len(extra))},
        compiler_params=_cparams("parallel", "parallel"),
        name="rglru_prompt",
    )(proj, proj, proj, proj, *weights, *extra)


def _rglru_sample_kernel(x_ref, g_ref, buf_ref, h0_ref, cw_ref, cb_ref, wr_ref, br_ref, wi_ref, bi_ref, lam_ref,
                         y_ref, hn_ref):
    xp = [buf_ref[j] for j in range(CONV_W - 1)] + [x_ref[t] for t in range(DEC_SEQ)]
    h = h0_ref[...]
    for t in range(DEC_SEQ):
        xc = _conv4(cw_ref, cb_ref, xp[t:t + CONV_W])
        a, mult, i = _rglru_terms(xc, wr_ref, br_ref, wi_ref, bi_ref, lam_ref)
        h = a * h + mult * i * xc
        y_ref[t] = (h * jax.nn.gelu(g_ref[t])).astype(BF16)
    hn_ref[...] = h


def _rglru_sample(xg_tm, buf_tm, h_all, weights, e, h_prev):
    ng = W_A // CG
    n_in = 11
    extra = [] if h_prev is None else [h_prev]
    state = pl.BlockSpec((None, DEC_BATCH, CG), lambda c: (e, 0, c))
    return pl.pallas_call(
        _passthrough(_rglru_sample_kernel, n_in, len(extra)),
        grid=(ng,),
        in_specs=[pl.BlockSpec((DEC_SEQ, DEC_BATCH, CG), lambda c: (0, 0, c)),
                  pl.BlockSpec((DEC_SEQ, DEC_BATCH, CG), lambda c: (0, 0, ng + c)),
                  pl.BlockSpec((CONV_W - 1, DEC_BATCH, CG), lambda c: (0, 0, c)),
                  state]
                 + _rglru_weight_specs(lambda c: c) + [_ANY] * len(extra),
        out_specs=[pl.BlockSpec((DEC_SEQ, DEC_BATCH, CG), lambda c: (0, 0, c)), state],
        out_shape=[jax.ShapeDtypeStruct((DEC_SEQ, DEC_BATCH, W_A), BF16),
                   jax.ShapeDtypeStruct((N_EVEN, DEC_BATCH, W_A), F32)],
        input_output_aliases={n_in + i: 1 + i for i in range(len(extra))},
        compiler_params=_cparams("parallel"),
        name="rglru_sample",
    )(xg_tm, xg_tm, buf_tm, h_all, *weights, *extra)


HG_ROWS = 128
DEC_PAD = SUB


def _lb_kernel(raw_ref, o_ref):
    raw = raw_ref[...]
    e = jnp.exp(raw - jnp.max(raw, axis=0, keepdims=True))
    p = e / jnp.sum(e, axis=0, keepdims=True)
    acc = jnp.zeros((1, F_B), F32)
    for r in range(N_EVEN):
        acc = acc + p[r:r + 1, :]
        o_ref[r:r + 1, :] = acc - p[0:1, :]


def _hgrn_lower_bounds(raw):
    return pl.pallas_call(_lb_kernel, out_shape=jax.ShapeDtypeStruct((N_EVEN, F_B), F32),
                          name="hgrn_lower_bounds")(raw)


def _hgrn_chunk(qb, fb, v, lb, st, n_valid, transposed=True):
    c = qb.shape[0]
    rows = lax.broadcasted_iota(jnp.int32, (c, DK_B), 0)
    f = lb + (1.0 - lb) * jax.nn.sigmoid(fb)
    kk = (1.0 - lb) * jax.nn.sigmoid(-fb)
    q = jax.nn.silu(qb)
    cum = _cumsum_rows(jnp.log(f))
    last = cum[n_valid - 1:n_valid, :]
    kdec = kk * jnp.exp(last - cum)
    if n_valid < c:
        kdec = jnp.where(rows < n_valid, kdec, 0.0)
    qdec = (q * jnp.exp(cum)).astype(BF16)
    if transposed:
        o_inter = _dot(qdec, st.astype(BF16), _NT)
        st_new = st * jnp.exp(last) + _dot(v.astype(BF16), kdec.astype(BF16), _TN)
    else:
        o_inter = _dot(qdec, st.astype(BF16))
        decay_col = jnp.broadcast_to(jnp.exp(last), (SUB, DK_B)).T[:, 0:1]
        st_new = st * decay_col + _dot(kdec.astype(BF16), v.astype(BF16), _TN)

    rows8 = lax.broadcasted_iota(jnp.int32, (SUB, DK_B), 0)
    cum2 = cum * LOG2E
    src2 = cum2 - jnp.log2(kk)
    o = o_inter
    for s in range(n_valid):
        t0 = (s // SUB) * SUB
        d = cum2[t0:c, :] - src2[s:s + 1, :]
        head = jnp.where(rows8 >= s - t0, d[0:SUB, :], NEG)
        d = head if t0 + SUB == c else jnp.concatenate([head, d[SUB:, :]], axis=0)
        p = jnp.exp2(d) * q[t0:c, :]
        term = jnp.sum(p, axis=1, keepdims=True) * v[s:s + 1, :]
        o = term + o if t0 == 0 else jnp.concatenate([o[0:t0, :], term + o[t0:c, :]], axis=0)
    return o, st_new


def _hgrn_finish(o, nb_ref, g_ref, y_ref):
    y_ref[...] = (_rms_scale(o) * nb_ref[...] * jax.nn.silu(g_ref[...])).astype(y_ref.dtype)


def _hgrn_prompt_kernel(q_ref, f_ref, i_ref, g_ref, qe_ref, fe_ref, ie_ref, ge_ref, lb_ref, nb_ref,
                        ym_ref, ye_ref, s_ref, st_scr, o_scr):
    j = pl.program_id(1)

    def run_rows(qr, fr, ir, n_rows, chunk):
        def head(h, carry):
            lanes = pl.ds(pl.multiple_of(h * DK_B, DK_B), DK_B)
            lb = lb_ref[:, lanes]
            st = st_scr[h]
            for c0 in range(0, n_rows, chunk):
                rs = slice(c0, c0 + chunk)
                o, st = _hgrn_chunk(qr[rs, lanes], fr[rs, lanes], ir[rs, lanes], lb, st, chunk)
                o_scr[rs, lanes] = o
            st_scr[h] = st
            return carry
        lax.fori_loop(0, H_B, head, 0, unroll=2)

    @pl.when(j == 0)
    def _():
        st_scr[...] = jnp.zeros(st_scr.shape, F32)
        run_rows(qe_ref, fe_ref, ie_ref, N_META, N_META)
        _hgrn_finish(o_scr[0:N_META, :], nb_ref, ge_ref, ye_ref)

    run_rows(q_ref, f_ref, i_ref, HG_ROWS, HGRN_CHUNK)
    _hgrn_finish(o_scr[...], nb_ref, g_ref, ym_ref)

    @pl.when(j == pl.num_programs(1) - 1)
    def _():
        def head(h, carry):
            s_ref[h] = st_scr[h].T
            return carry
        lax.fori_loop(0, H_B, head, 0)


def _hgrn_prompt(proj, y, lb, norm_b, e, s_prev):
    nm = ROW_META0 // N_META
    nj = SEQ // HG_ROWS
    main = lambda col: pl.BlockSpec((HG_ROWS, W_B), lambda b, j: (b * nj + j, col))
    meta = lambda col: pl.BlockSpec((N_META, W_B), lambda b, j: (nm + b, col))
    vec = pl.BlockSpec((1, W_B), lambda b, j: (0, 0))
    n_in = 10
    extra = [y] + ([] if s_prev is None else [s_prev])
    aliases = {n_in: 0}
    if s_prev is not None:
        aliases[n_in + 1] = 2
    return pl.pallas_call(
        _passthrough(_hgrn_prompt_kernel, n_in, len(extra)),
        grid=(BATCH, nj),
        in_specs=[main(2), main(3), main(4), main(5), meta(2), meta(3), meta(4), meta(5), vec, vec]
                 + [_ANY] * len(extra),
        out_specs=[pl.BlockSpec((HG_ROWS, W_B), lambda b, j: (b * nj + j, W_A // W_B)),
                   pl.BlockSpec((N_META, W_B), lambda b, j: (b, 0)),
                   pl.BlockSpec((None, None, H_B, DK_B, DV_B), lambda b, j: (e, b, 0, 0, 0))],
        out_shape=[jax.ShapeDtypeStruct((ROWS, D_MODEL), BF16),
                   jax.ShapeDtypeStruct((ROWS_META, W_B), BF16),
                   jax.ShapeDtypeStruct((N_EVEN, BATCH, H_B, DK_B, DV_B), F32)],
        scratch_shapes=[pltpu.VMEM((H_B, DV_B, DK_B), F32), pltpu.VMEM((HG_ROWS, W_B), F32)],
        input_output_aliases=aliases,
        compiler_params=_cparams("parallel", "arbitrary"),
        name="hgrn_prompt",
    )(*([proj] * 8), lb.reshape(1, F_B), norm_b.reshape(1, W_B), *extra)


def _hgrn_sample_kernel(q_ref, f_ref, i_ref, g_ref, lb_ref, nb_ref, s_in_ref, y_ref, s_out_ref, o_scr):
    for h in range(H_B):
        lanes = slice(h * DK_B, (h + 1) * DK_B)
        o, st = _hgrn_chunk(q_ref[:, lanes], f_ref[:, lanes], i_ref[:, lanes], lb_ref[:, lanes], s_in_ref[h],
                            DEC_SEQ, transposed=False)
        o_scr[:, lanes] = o
        s_out_ref[h] = st
    _hgrn_finish(o_scr[...], nb_ref, g_ref, y_ref)


def _hgrn_sample(proj_s, lb, norm_b, s_all, e, s_prev):
    part = lambda col: pl.BlockSpec((None, DEC_PAD, W_B), lambda b: (b, 0, col))
    vec = pl.BlockSpec((1, W_B), lambda b: (0, 0))
    state = pl.BlockSpec((None, None, H_B, DK_B, DV_B), lambda b: (e, b, 0, 0, 0))
    n_in = 7
    extra = [] if s_prev is None else [s_prev]
    return pl.pallas_call(
        _passthrough(_hgrn_sample_kernel, n_in, len(extra)),
        grid=(DEC_BATCH,),
        in_specs=[part(2), part(3), part(4), part(5), vec, vec, state] + [_ANY] * len(extra),
        out_specs=[pl.BlockSpec((None, DEC_PAD, W_B), lambda b: (b, 0, 0)), state],
        out_shape=[jax.ShapeDtypeStruct((DEC_BATCH, DEC_PAD, W_B), F32),
                   jax.ShapeDtypeStruct((N_EVEN, DEC_BATCH, H_B, DK_B, DV_B), F32)],
        scratch_shapes=[pltpu.VMEM((DEC_PAD, W_B), F32)],
        input_output_aliases={n_in + i: 1 + i for i in range(len(extra))},
        compiler_params=_cparams("parallel"),
        name="hgrn_sample",
    )(*([proj_s] * 4), lb.reshape(1, F_B), norm_b.reshape(1, W_B), s_all, *extra)


MLSTM_GROUP_PROMPT = 2


def _mlstm_gates(ig, fg, m_prev, n_valid):
    c = ig.shape[0]
    ti = lax.broadcasted_iota(jnp.int32, (c, c), 0)
    si = lax.broadcasted_iota(jnp.int32, (c, c), 1)
    tri = si <= ti
    to_row = lambda col: jnp.sum(jnp.where(ti == si, col, 0.0), axis=0, keepdims=True)
    cum = jnp.sum(jnp.where(tri, to_row(jax.nn.log_sigmoid(fg)), 0.0), axis=1, keepdims=True)
    logw = jnp.where(tri, cum - to_row(cum) + to_row(ig), NEG)
    log_inter = cum + m_prev
    m_t = jnp.maximum(log_inter, jnp.max(logw, axis=1, keepdims=True))
    w = jnp.exp(logw - m_t)
    g = jnp.exp(log_inter - m_t)
    lv = n_valid - 1
    m_new = m_t[lv:lv + 1, :]
    cum_last = cum[lv:lv + 1, :]
    ws = jnp.exp(cum_last - cum + ig - m_new)
    if n_valid < c:
        ws = jnp.where(lax.broadcasted_iota(jnp.int32, (c, 1), 0) < n_valid, ws, 0.0)
    decay = jnp.exp(cum_last + m_prev - m_new)
    return w, g, m_t, ws, decay, m_new


def _mlstm_heads(q_ref, k_ref, v_ref, o_ref, g_ref, bias_ref, nc_ref, y_ref, get_state, put_state, n_valid, group):
    for h0 in range(0, H_C, group):
        heads = range(h0, h0 + group)
        kcols = {h: slice(h * DK_C, (h + 1) * DK_C) for h in heads}
        vcols = {h: slice(h * DV_C, (h + 1) * DV_C) for h in heads}
        state = {h: get_state(h) for h in heads}
        gates = {}
        for h in heads:
            ig = g_ref[:, h:h + 1] + bias_ref[0:1, h:h + 1]
            fg = g_ref[:, H_C + h:H_C + h + 1] + bias_ref[0:1, H_C + h:H_C + h + 1]
            gates[h] = _mlstm_gates(ig, fg, state[h][2], n_valid)
        qs = {h: q_ref[:, kcols[h]] * (DK_C ** -0.5) for h in heads}
        qb = {h: qs[h].astype(BF16) for h in heads}
        vb = {h: v_ref[:, vcols[h]].astype(BF16) for h in heads}
        qk = {h: _dot(qb[h], k_ref[:, kcols[h]].astype(BF16), _NT) for h in heads}
        inter = {h: _dot(qb[h], state[h][0].astype(BF16)) for h in heads}
        p = {h: qk[h] * gates[h][0] for h in heads}
        intra = {h: _dot(p[h].astype(BF16), vb[h]) for h in heads}
        for h in heads:
            w, g, m_t, ws, decay, m_new = gates[h]
            num = g * inter[h] + intra[h]
            den = g * jnp.sum(qs[h] * state[h][1], axis=1, keepdims=True) + jnp.sum(p[h], axis=1, keepdims=True)
            hh = num / jnp.maximum(jnp.abs(den), jnp.exp(-m_t))
            hn = _rms_scale(hh) * nc_ref[:, vcols[h]]
            y_ref[:, vcols[h]] = (jax.nn.sigmoid(o_ref[:, vcols[h]]) * hn).astype(y_ref.dtype)
        kw = {h: k_ref[:, kcols[h]] * gates[h][3] for h in heads}
        upd = {h: _dot(kw[h].astype(BF16), vb[h], _TN) for h in heads}
        for h in heads:
            C, n_row, _ = state[h]
            decay, m_new = gates[h][4], gates[h][5]
            put_state(h, decay * C + upd[h], decay * n_row + jnp.sum(kw[h], axis=0, keepdims=True), m_new)


def _mlstm_prompt_kernel(q_ref, k_ref, v_ref, o_ref, g_ref, qe_ref, ke_ref, ve_ref, oe_ref, ge_ref, bias_ref, nc_ref,
                         ym_ref, ye_ref, c_ref, n_ref, m_ref):
    def get_state(h):
        return c_ref[h], n_ref[h:h + 1, :], m_ref[0:1, h:h + 1]

    def put_state(h, C, n_row, m_new):
        c_ref[h] = C
        n_ref[h:h + 1, :] = n_row
        m_ref[0:1, h:h + 1] = m_new

    @pl.when(pl.program_id(1) == 0)
    def _():
        c_ref[...] = jnp.zeros(c_ref.shape, F32)
        n_ref[...] = jnp.zeros(n_ref.shape, F32)
        m_ref[...] = jnp.zeros(m_ref.shape, F32)
        _mlstm_heads(qe_ref, ke_ref, ve_ref, oe_ref, ge_ref, bias_ref, nc_ref, ye_ref, get_state, put_state, N_META,
                     MLSTM_GROUP_PROMPT)

    _mlstm_heads(q_ref, k_ref, v_ref, o_ref, g_ref, bias_ref, nc_ref, ym_ref, get_state, put_state, MLSTM_CHUNK,
                 MLSTM_GROUP_PROMPT)


def _mlstm_state_specs(d, index):
    return [pl.BlockSpec((None, None, H_C, DK_C, DV_C), lambda *g: (d, index(*g), 0, 0, 0)),
            pl.BlockSpec((None, None, H_C, DK_C), lambda *g: (d, index(*g), 0, 0)),
            pl.BlockSpec((None, None, 1, H_C), lambda *g: (d, index(*g), 0, 0))]


def _mlstm_state_shapes(batch):
    return [jax.ShapeDtypeStruct((N_ODD, batch, H_C, DK_C, DV_C), F32),
            jax.ShapeDtypeStruct((N_ODD, batch, H_C, DK_C), F32),
            jax.ShapeDtypeStruct((N_ODD, batch, 1, H_C), F32)]


def _mlstm_prompt(proj, gates, bias, norm_c, d, prev):
    nm = ROW_META0 // N_META
    nj = SEQ // MLSTM_CHUNK
    main = lambda width, col: pl.BlockSpec((MLSTM_CHUNK, width), lambda b, j: (b * nj + j, col))
    meta = lambda width, col: pl.BlockSpec((N_META, width), lambda b, j: (nm + b, col))
    n_in = 12
    extra = [] if prev is None else list(prev)
    return pl.pallas_call(
        _passthrough(_mlstm_prompt_kernel, n_in, len(extra)),
        grid=(BATCH, nj),
        in_specs=[main(W_CK, 0), main(W_CK, 1), main(W_CV, 1), main(W_CV, 2), main(LANES, 0),
                  meta(W_CK, 0), meta(W_CK, 1), meta(W_CV, 1), meta(W_CV, 2), meta(LANES, 0),
                  pl.BlockSpec((1, LANES), lambda b, j: (0, 0)), pl.BlockSpec((1, W_CV), lambda b, j: (0, 0))]
                 + [_ANY] * len(extra),
        out_specs=[pl.BlockSpec((MLSTM_CHUNK, W_CV), lambda b, j: (b * nj + j, 0)),
                   pl.BlockSpec((N_META, W_CV), lambda b, j: (b, 0))]
                  + _mlstm_state_specs(d, lambda b, j: b),
        out_shape=[jax.ShapeDtypeStruct((ROWS, W_CV), BF16),
                   jax.ShapeDtypeStruct((ROWS_META, W_CV), BF16)] + _mlstm_state_shapes(BATCH),
        input_output_aliases={n_in + i: 2 + i for i in range(len(extra))},
        compiler_params=_cparams("parallel", "arbitrary"),
        name="mlstm_prompt",
    )(proj, proj, proj, proj, gates, proj, proj, proj, proj, gates, bias, norm_c.reshape(1, W_CV), *extra)


def _mlstm_sample_kernel(q_ref, k_ref, v_ref, o_ref, g_ref, bias_ref, nc_ref, c0_ref, n0_ref, m0_ref,
                         y_ref, c_ref, n_ref, m_ref):
    def get_state(h):
        return c0_ref[h], n0_ref[h:h + 1, :], m0_ref[0:1, h:h + 1]

    def put_state(h, C, n_row, m_new):
        c_ref[h] = C
        n_ref[h:h + 1, :] = n_row
        m_ref[0:1, h:h + 1] = m_new

    _mlstm_heads(q_ref, k_ref, v_ref, o_ref, g_ref, bias_ref, nc_ref, y_ref, get_state, put_state, DEC_SEQ, H_C)


def _mlstm_sample(proj_s, gates_s, bias, norm_c, c_all, n_all, m_all, d, prev):
    part = lambda width, col: pl.BlockSpec((None, DEC_PAD, width), lambda b: (b, 0, col))
    states = _mlstm_state_specs(d, lambda b: b)
    n_in = 10
    extra = [] if prev is None else list(prev)
    return pl.pallas_call(
        _passthrough(_mlstm_sample_kernel, n_in, len(extra)),
        grid=(DEC_BATCH,),
        in_specs=[part(W_CK, 0), part(W_CK, 1), part(W_CV, 1), part(W_CV, 2), part(LANES, 0),
                  pl.BlockSpec((1, LANES), lambda b: (0, 0)), pl.BlockSpec((1, W_CV), lambda b: (0, 0))]
                 + states + [_ANY] * len(extra),
        out_specs=[pl.BlockSpec((None, DEC_PAD, W_CV), lambda b: (b, 0, 0))] + states,
        out_shape=[jax.ShapeDtypeStruct((DEC_BATCH, DEC_PAD, W_CV), F32)] + _mlstm_state_shapes(DEC_BATCH),
        input_output_aliases={n_in + i: 1 + i for i in range(len(extra))},
        compiler_params=_cparams("parallel"),
        name="mlstm_sample",
    )(proj_s, proj_s, proj_s, proj_s, gates_s, bias, norm_c.reshape(1, W_CV), c_all, n_all, m_all, *extra)


def _sample_rows(a):
    return a[ROW_S0:ROW_META0].reshape(DEC_BATCH, DEC_SEQ, a.shape[-1])


def _pad_steps(a):
    return jnp.pad(a, ((0, 0), (0, DEC_PAD - DEC_SEQ), (0, 0)))


def _fill_tail_rows(y, sample, meta):
    cat = lambda parts: parts[0] if len(parts) == 1 else jnp.concatenate(parts, axis=1)
    tail = jnp.concatenate([cat(sample), cat(meta), jnp.zeros((ROWS_PAD, y.shape[1]), y.dtype)], axis=0)
    return lax.dynamic_update_slice(y, tail, (ROWS_P, 0))


def _mix_even(proj, e, conv_all, h_all, s_all, rg_weights, lb, norm_b, prev):
    p_h0, p_s0, s_h0, s_s0 = prev if prev is not None else (None,) * 4
    y, ya_e, p_h = _rglru_prompt(proj, rg_weights, e, p_h0)
    y, yb_e, p_s = _hgrn_prompt(proj, y, lb, norm_b, e, p_s0)
    ps = _sample_rows(proj)
    xg_tm = jnp.swapaxes(ps[:, :, :2 * W_A], 0, 1)
    ya_s, s_h = _rglru_sample(xg_tm, jnp.swapaxes(conv_all[e], 0, 1), h_all, rg_weights, e, s_h0)
    ya_s = jnp.swapaxes(ya_s, 0, 1).reshape(ROWS_S, W_A)
    yb_s, s_s = _hgrn_sample(_pad_steps(ps), lb, norm_b, s_all, e, s_s0)
    yb_s = yb_s[:, :DEC_SEQ].reshape(ROWS_S, W_B).astype(BF16)
    y = _fill_tail_rows(y, [ya_s, yb_s], [ya_e, yb_e])
    conv_p = jnp.stack([proj[(b + 1) * SEQ - (CONV_W - 1):(b + 1) * SEQ, :W_A] for b in range(BATCH)])
    conv_s = ps[:, DEC_SEQ - (CONV_W - 1):, :W_A]
    return y, conv_p, conv_s, (p_h, p_s, s_h, s_s)


def _mix_odd(proj, gates, bias, norm_c, d, c_all, n_all, m_all, prev):
    p_prev, s_prev = (prev[:3], prev[3:]) if prev is not None else (None, None)
    y, y_e, *p_state = _mlstm_prompt(proj, gates, bias, norm_c, d, p_prev)
    y_s, *s_state = _mlstm_sample(_pad_steps(_sample_rows(proj)), _pad_steps(_sample_rows(gates)), bias, norm_c,
                                  c_all, n_all, m_all, d, s_prev)
    y_s = y_s[:, :DEC_SEQ].reshape(ROWS_S, W_CV).astype(BF16)
    y = _fill_tail_rows(y, [y_s], [y_e])
    return y, (*p_state, *s_state)


def kernel(x_prompt, x_sample, state_rglru_conv, state_rglru_h, state_hgrn_S, state_mlstm_C, state_mlstm_n, state_mlstm_m, meta_tokens, ffn1_norm, ffn1_w_gate, ffn1_w_up, ffn1_w_down, mix_norm, ffn2_norm, ffn2_w_gate, ffn2_w_up, ffn2_w_down, even_w_in, rglru_conv_w, rglru_conv_b, rglru_w_r, rglru_b_r, rglru_w_i, rglru_b_i, rglru_lambda, hgrn_lb_raw, hgrn_norm, even_w_out, odd_w_in, mlstm_b_gates, mlstm_norm, odd_w_out, final_norm):
    meta = jnp.broadcast_to(meta_tokens[None], (BATCH, N_META, D_MODEL)).reshape(ROWS_META, D_MODEL)
    x = jnp.concatenate([x_prompt.reshape(ROWS_P, D_MODEL), x_sample.reshape(ROWS_S, D_MODEL), meta,
                         jnp.zeros((ROWS_PAD, D_MODEL), F32)], axis=0)
    lb_all = _hgrn_lower_bounds(hgrn_lb_raw)
    m_all = state_mlstm_m.reshape(N_ODD, DEC_BATCH, 1, H_C)
    odd_w_in_t = jnp.swapaxes(odd_w_in, 1, 2)
    p_conv, s_conv = [], []
    even_states = odd_states = None

    for l in range(DEPTH):
        x = _ffn_half(x, ffn1_norm, ffn1_w_gate, ffn1_w_up, ffn1_w_down, l)
        u = _rmsnorm(x, mix_norm[l], BF16)
        if l % 2 == 0:
            e = l // 2
            proj = _matmul(u, even_w_in, e, IN_EVEN)
            rg_weights = _rglru_weights(rglru_conv_w[e], rglru_conv_b[e], rglru_w_r[e], rglru_b_r[e], rglru_w_i[e],
                                        rglru_b_i[e], rglru_lambda[e])
            y, conv_p, conv_s, even_states = _mix_even(proj, e, state_rglru_conv, state_rglru_h, state_hgrn_S,
                                                       rg_weights, lb_all[e], hgrn_norm[e], even_states)
            p_conv.append(conv_p)
            s_conv.append(conv_s)
            x = _matmul_residual(y, even_w_out, e, x, 1.0)
        else:
            d = l // 2
            proj = _matmul_t(u, odd_w_in_t, d, IN_ODD_MAIN)
            gates = _matmul_tail(u, odd_w_in_t, d, IN_ODD_MAIN)
            bias = jnp.pad(mlstm_b_gates[d].reshape(1, 2 * H_C), ((0, 0), (0, LANES - 2 * H_C)))
            y, odd_states = _mix_odd(proj, gates, bias, mlstm_norm[d], d, state_mlstm_C, state_mlstm_n, m_all,
                                     odd_states)
            x = _matmul_residual(y, odd_w_out, d, x, 1.0)
        x = _ffn_half(x, ffn2_norm, ffn2_w_gate, ffn2_w_up, ffn2_w_down, l)

    y_prompt = _rmsnorm(x, final_norm, F32, TR_OUT, 0, ROWS_P).reshape(BATCH, SEQ, D_MODEL)
    y_sample = _rmsnorm(x, final_norm, F32, TR_OUT, ROW_S0, ROWS_S).reshape(DEC_BATCH, DEC_SEQ, D_MODEL)
    p_h, p_s, s_h, s_s = even_states
    p_c, p_n, p_m, s_c, s_n, s_m = odd_states
    return (y_prompt, y_sample,
            jnp.stack(p_conv), p_h.reshape(N_EVEN, BATCH, W_A), p_s, p_c, p_n, p_m.reshape(N_ODD, BATCH, H_C),
            jnp.stack(s_conv), s_h, s_s, s_c, s_n, s_m.reshape(N_ODD, DEC_BATCH, H_C))
```

```python
import functools

import jax
import jax.numpy as jnp
from jax import lax
from jax.experimental import pallas as pl
from jax.experimental.pallas import tpu as pltpu

F32 = jnp.float32
BF16 = jnp.bfloat16

D_MODEL = 4096
BATCH = 4
SEQ = 2048
DEPTH = 4
DEC_BATCH = 128
DEC_SEQ = 4
N_META = 16
N_EVEN = (DEPTH + 1) // 2
N_ODD = DEPTH // 2
D_FF = 2 * D_MODEL
EPS = 1e-6
W_A = D_MODEL // 2
H_A = 16
BW_A = W_A // H_A
CONV_W = 4
RG_C = 8.0
W_B = D_MODEL // 2
DK_B = 128
H_B = W_B // DK_B
DV_B = W_B // H_B
F_B = H_B * DK_B
HGRN_CHUNK = 32
H_C = 8
DV_C = D_MODEL // H_C
DK_C = DV_C // 2
W_CV = H_C * DV_C
W_CK = H_C * DK_C
MLSTM_CHUNK = 128
IN_EVEN = 2 * W_A + 2 * F_B + 2 * W_B
IN_ODD_MAIN = 2 * W_CK + 2 * W_CV

ROWS_P = BATCH * SEQ
ROWS_S = DEC_BATCH * DEC_SEQ
ROWS_META = BATCH * N_META
ROW_S0 = ROWS_P
ROW_META0 = ROWS_P + ROWS_S
ROWS_USED = ROW_META0 + ROWS_META
ROWS = 8832
ROWS_PAD = ROWS - ROWS_USED

LANES = 128
SUB = 8
VMEM_LIMIT = 60 * 1024 * 1024

TM = 1472
TM_TALL = 2944
TN = 256
TN_WIDE = 512
TR_NORM = 736
TR_OUT = 512


def _cparams(*sem):
    return pltpu.CompilerParams(dimension_semantics=sem, vmem_limit_bytes=VMEM_LIMIT)


def _rmsnorm_kernel(x_ref, g_ref, o_ref):
    x = x_ref[...]
    y = x * lax.rsqrt(jnp.mean(x * x, axis=-1, keepdims=True) + EPS)
    o_ref[...] = (y * g_ref[...]).astype(o_ref.dtype)


def _rmsnorm(x, g, out_dtype, tile=TR_NORM, row0=0, rows=ROWS):
    d = x.shape[1]
    first = row0 // tile
    return pl.pallas_call(
        _rmsnorm_kernel,
        grid=(rows // tile,),
        in_specs=[pl.BlockSpec((tile, d), lambda i: (first + i, 0)),
                  pl.BlockSpec((1, d), lambda i: (0, 0))],
        out_specs=pl.BlockSpec((tile, d), lambda i: (i, 0)),
        out_shape=jax.ShapeDtypeStruct((rows, d), out_dtype),
        compiler_params=_cparams("parallel"),
        name="rmsnorm",
    )(x, g.reshape(1, d))


def _mm_kernel(a_ref, w_ref, o_ref):
    o_ref[...] = jnp.dot(a_ref[...], w_ref[...].astype(BF16), preferred_element_type=F32)


def _matmul(a, w3, layer, n_cols):
    rows, k = a.shape
    return pl.pallas_call(
        _mm_kernel,
        grid=(rows // TM, n_cols // TN_WIDE),
        in_specs=[pl.BlockSpec((TM, k), lambda m, n: (m, 0)),
                  pl.BlockSpec((None, k, TN_WIDE), lambda m, n: (layer, 0, n))],
        out_specs=pl.BlockSpec((TM, TN_WIDE), lambda m, n: (m, n)),
        out_shape=jax.ShapeDtypeStruct((rows, n_cols), F32),
        compiler_params=_cparams("parallel", "arbitrary"),
        name="matmul",
    )(a, w3)


_NT = (((1,), (1,)), ((), ()))
_TN = (((0,), (0,)), ((), ()))


def _mm_t_kernel(a_ref, wt_ref, o_ref):
    o_ref[...] = lax.dot_general(a_ref[...], wt_ref[...].astype(BF16), _NT, preferred_element_type=F32)


def _matmul_t(a, w3t, layer, n_cols):
    rows, k = a.shape
    return pl.pallas_call(
        _mm_t_kernel,
        grid=(rows // TM, n_cols // TN_WIDE),
        in_specs=[pl.BlockSpec((TM, k), lambda m, n: (m, 0)),
                  pl.BlockSpec((None, TN_WIDE, k), lambda m, n: (layer, n, 0))],
        out_specs=pl.BlockSpec((TM, TN_WIDE), lambda m, n: (m, n)),
        out_shape=jax.ShapeDtypeStruct((rows, n_cols), F32),
        compiler_params=_cparams("parallel", "arbitrary"),
        name="matmul_t",
    )(a, w3t)


def _mm_tail_kernel(a_ref, wt_ref, o_ref, *, n_valid):
    row = lax.broadcasted_iota(jnp.int32, wt_ref.shape, 0)
    wt = jnp.where(row < n_valid, wt_ref[...], 0.0).astype(BF16)
    o_ref[...] = lax.dot_general(a_ref[...], wt, _NT, preferred_element_type=F32)


def _matmul_tail(a, w3t, layer, row0):
    rows, k = a.shape
    n_valid = w3t.shape[1] - row0
    return pl.pallas_call(
        functools.partial(_mm_tail_kernel, n_valid=n_valid),
        grid=(rows // TM,),
        in_specs=[pl.BlockSpec((TM, k), lambda m: (m, 0)),
                  pl.BlockSpec((None, LANES, k), lambda m: (layer, row0 // LANES, 0))],
        out_specs=pl.BlockSpec((TM, LANES), lambda m: (m, 0)),
        out_shape=jax.ShapeDtypeStruct((rows, LANES), F32),
        compiler_params=_cparams("parallel"),
        name="matmul_tail",
    )(a, w3t)


def _gateup_kernel(a_ref, wg_ref, wu_ref, o_ref):
    a = a_ref[...]
    g = jnp.dot(a, wg_ref[...].astype(BF16), preferred_element_type=F32)
    u = jnp.dot(a, wu_ref[...].astype(BF16), preferred_element_type=F32)
    o_ref[...] = (jax.nn.silu(g) * u).astype(o_ref.dtype)


def _gateup(a, wg, wu, layer):
    rows, k = a.shape
    n = wg.shape[-1]
    wspec = pl.BlockSpec((None, k, TN), lambda m, j: (layer, 0, j))
    a_spec = pl.BlockSpec((TM_TALL, k), lambda m, j: (m, 0), pipeline_mode=pl.Buffered(1))
    return pl.pallas_call(
        _gateup_kernel,
        grid=(rows // TM_TALL, n // TN),
        in_specs=[a_spec, wspec, wspec],
        out_specs=pl.BlockSpec((TM_TALL, TN), lambda m, j: (m, j)),
        out_shape=jax.ShapeDtypeStruct((rows, n), BF16),
        compiler_params=_cparams("parallel", "arbitrary"),
        name="gateup",
    )(a, wg, wu)


def _mm_res_kernel(a_ref, w_ref, x_ref, o_ref, *, scale):
    acc = jnp.dot(a_ref[...], w_ref[...].astype(BF16), preferred_element_type=F32)
    o_ref[...] = x_ref[...] + scale * acc


def _matmul_residual(a, w3, layer, x, scale):
    rows, k = a.shape
    n = w3.shape[-1]
    wide = k > D_MODEL
    a_mode = dict(pipeline_mode=pl.Buffered(1)) if wide else {}
    tn = TN if wide else TN_WIDE
    return pl.pallas_call(
        functools.partial(_mm_res_kernel, scale=scale),
        grid=(rows // TM, n // tn),
        in_specs=[pl.BlockSpec((TM, k), lambda m, j: (m, 0), **a_mode),
                  pl.BlockSpec((None, k, tn), lambda m, j: (layer, 0, j)),
                  pl.BlockSpec((TM, tn), lambda m, j: (m, j))],
        out_specs=pl.BlockSpec((TM, tn), lambda m, j: (m, j)),
        out_shape=jax.ShapeDtypeStruct((rows, n), F32),
        input_output_aliases={2: 0},
        compiler_params=_cparams("parallel", "arbitrary"),
        name="matmul_residual",
    )(a, w3, x)


def _ffn_half(x, norm_g, wg, wu, wd, layer):
    xn = _rmsnorm(x, norm_g[layer], BF16)
    h = _gateup(xn, wg, wu, layer)
    return _matmul_residual(h, wd, layer, x, 0.5)


NEG = -1e30
LOG2E = 1.4426950408889634


def _dot(a, b, dims=None):
    if dims is None:
        return jnp.dot(a, b, preferred_element_type=F32)
    return lax.dot_general(a, b, dims, preferred_element_type=F32)


def _expm1_nonpos(x):
    u = jnp.exp(x)
    um1 = u - 1.0
    y = um1 * x / jnp.where(u == 1.0, 1.0, jnp.log(u))
    y = jnp.where(u == 1.0, x, y)
    return jnp.where(um1 == -1.0, -1.0, y)


def _rms_scale(x):
    return x * lax.rsqrt(jnp.mean(x * x, axis=-1, keepdims=True) + EPS)


def _passthrough(body, n_in, n_extra):
    def wrapped(*refs):
        return body(*refs[:n_in], *refs[n_in + n_extra:])
    return wrapped


_ANY = pl.BlockSpec(memory_space=pl.ANY)


def _pad_rows(x, rows):
    if x.shape[0] == rows:
        return x
    return jnp.concatenate([x, jnp.zeros((rows - x.shape[0], x.shape[1]), x.dtype)], axis=0)


def _cumsum_rows(x):
    rows = lax.broadcasted_iota(jnp.int32, x.shape, 0)
    d = 1
    while d < x.shape[0]:
        x = x + jnp.where(rows >= d, pltpu.roll(x, d, 0), 0.0)
        d *= 2
    return x


CG = 512
HPG = CG // BW_A
RG_ROWS = N_META + SEQ
RG_PAD = SUB
RG_CHUNK = 256


def _rglru_terms(xc, wr_ref, br_ref, wi_ref, bi_ref, lam_ref):
    xcb = xc.astype(BF16)
    rs, gs = [], []
    for h in range(HPG):
        xh = xcb[:, h * BW_A:(h + 1) * BW_A]
        rs.append(_dot(xh, wr_ref[h].astype(BF16)))
        gs.append(_dot(xh, wi_ref[h].astype(BF16)))
    r = jax.nn.sigmoid(jnp.concatenate(rs, axis=1) + br_ref[...])
    i = jax.nn.sigmoid(jnp.concatenate(gs, axis=1) + bi_ref[...])
    log_a = -RG_C * r * jax.nn.softplus(-lam_ref[...])
    a = jnp.exp(log_a)
    mult = jnp.sqrt(-_expm1_nonpos(2.0 * log_a))
    return a, mult, i


def _conv4(cw_ref, cb_ref, taps):
    y = cb_ref[...]
    for j in range(CONV_W):
        y = y + cw_ref[j:j + 1, :] * taps[j]
    return y


def _rglru_prompt_kernel(xm_ref, xe_ref, gm_ref, ge_ref, cw_ref, cb_ref, wr_ref, br_ref, wi_ref, bi_ref, lam_ref,
                         ym_ref, ye_ref, hl_ref, xs_scr, a_scr, b_scr):
    xs_scr[0:RG_PAD, :] = jnp.zeros((RG_PAD, CG), F32)
    xs_scr[RG_PAD:RG_PAD + N_META, :] = xe_ref[...]
    xs_scr[RG_PAD + N_META:, :] = xm_ref[...]
    chunks = [(0, N_META)] + [(N_META + j * RG_CHUNK, RG_CHUNK) for j in range(SEQ // RG_CHUNK)]
    for t0, n in chunks:
        taps = [xs_scr[RG_PAD + t0 - d:RG_PAD + t0 - d + n, :] for d in (3, 2, 1, 0)]
        xc = _conv4(cw_ref, cb_ref, taps)
        a, mult, i = _rglru_terms(xc, wr_ref, br_ref, wi_ref, bi_ref, lam_ref)
        if t0 == 0:
            mult = jnp.where(lax.broadcasted_iota(jnp.int32, (n, CG), 0) == 0, 1.0, mult)
        a_scr[t0:t0 + n, :] = a
        b_scr[t0:t0 + n, :] = mult * i * xc

    rows8 = lax.broadcasted_iota(jnp.int32, (SUB, CG), 0)

    def scan_block(blk, h_prev):
        r0 = pl.multiple_of(blk * SUB, SUB)
        a = a_scr[pl.ds(r0, SUB), :]
        b = b_scr[pl.ds(r0, SUB), :]
        for d in (1, 2, 4):
            keep = rows8 >= d
            b = jnp.where(keep, a * pltpu.roll(b, d, 0) + b, b)
            a = jnp.where(keep, a * pltpu.roll(a, d, 0), a)
        h = b + a * h_prev
        b_scr[pl.ds(r0, SUB), :] = h
        return h[SUB - 1:SUB, :]

    hl_ref[...] = lax.fori_loop(0, RG_ROWS // SUB, scan_block, jnp.zeros((1, CG), F32))
    ye_ref[...] = (b_scr[0:N_META, :] * jax.nn.gelu(ge_ref[...])).astype(BF16)
    for j in range(SEQ // RG_CHUNK):
        rows = slice(j * RG_CHUNK, (j + 1) * RG_CHUNK)
        h = b_scr[N_META + j * RG_CHUNK:N_META + (j + 1) * RG_CHUNK, :]
        ym_ref[rows, :] = (h * jax.nn.gelu(gm_ref[rows, :])).astype(BF16)


def _rglru_weight_specs(index):
    vec = pl.BlockSpec((1, CG), lambda *g: (0, index(*g)))
    gate = pl.BlockSpec((HPG, BW_A, BW_A), lambda *g: (index(*g), 0, 0))
    return [pl.BlockSpec((CONV_W, CG), lambda *g: (0, index(*g))), vec, gate, vec, gate, vec, vec]


def _rglru_weights(cw, cb, wr, br, wi, bi, lam):
    row = lambda a: a.reshape(1, W_A)
    return (cw, row(cb), wr, row(br), wi, row(bi), row(lam))


def _rglru_prompt(proj, weights, e, h_prev):
    nm = ROW_META0 // N_META
    ng = W_A // CG
    n_in = 11
    extra = [] if h_prev is None else [h_prev]
    return pl.pallas_call(
        _passthrough(_rglru_prompt_kernel, n_in, len(extra)),
        grid=(BATCH, ng),
        in_specs=[pl.BlockSpec((SEQ, CG), lambda b, c: (b, c)),
                  pl.BlockSpec((N_META, CG), lambda b, c: (nm + b, c)),
                  pl.BlockSpec((SEQ, CG), lambda b, c: (b, ng + c)),
                  pl.BlockSpec((N_META, CG), lambda b, c: (nm + b, ng + c))]
                 + _rglru_weight_specs(lambda b, c: c) + [_ANY] * len(extra),
        out_specs=[pl.BlockSpec((SEQ, CG), lambda b, c: (b, c)),
                   pl.BlockSpec((N_META, CG), lambda b, c: (b, c)),
                   pl.BlockSpec((None, None, 1, CG), lambda b, c: (e, b, 0, c))],
        out_shape=[jax.ShapeDtypeStruct((ROWS, D_MODEL), BF16),
                   jax.ShapeDtypeStruct((ROWS_META, W_A), BF16),
                   jax.ShapeDtypeStruct((N_EVEN, BATCH, 1, W_A), F32)],
        scratch_shapes=[pltpu.VMEM((RG_PAD + RG_ROWS, CG), F32), pltpu.VMEM((RG_ROWS, CG), F32),
                        pltpu.VMEM((RG_ROWS, CG), F32)],
        input_output_aliases={n_in + i: 2 + i for i in range(len(extra))},
        compiler_params=_cparams("parallel", "parallel"),
        name="rglru_prompt",
    )(proj, proj, proj, proj, *weights, *extra)


def _rglru_sample_kernel(x_ref, g_ref, buf_ref, h0_ref, cw_ref, cb_ref, wr_ref, br_ref, wi_ref, bi_ref, lam_ref,
                         y_ref, hn_ref):
    xp = [buf_ref[j] for j in range(CONV_W - 1)] + [x_ref[t] for t in range(DEC_SEQ)]
    h = h0_ref[...]
    for t in range(DEC_SEQ):
        xc = _conv4(cw_ref, cb_ref, xp[t:t + CONV_W])
        a, mult, i = _rglru_terms(xc, wr_ref, br_ref, wi_ref, bi_ref, lam_ref)
        h = a * h + mult * i * xc
        y_ref[t] = (h * jax.nn.gelu(g_ref[t])).astype(BF16)
    hn_ref[...] = h


def _rglru_sample(xg_tm, buf_tm, h_all, weights, e, h_prev):
    ng = W_A // CG
    n_in = 11
    extra = [] if h_prev is None else [h_prev]
    state = pl.BlockSpec((None, DEC_BATCH, CG), lambda c: (e, 0, c))
    return pl.pallas_call(
        _passthrough(_rglru_sample_kernel, n_in, len(extra)),
        grid=(ng,),
        in_specs=[pl.BlockSpec((DEC_SEQ, DEC_BATCH, CG), lambda c: (0, 0, c)),
                  pl.BlockSpec((DEC_SEQ, DEC_BATCH, CG), lambda c: (0, 0, ng + c)),
                  pl.BlockSpec((CONV_W - 1, DEC_BATCH, CG), lambda c: (0, 0, c)),
                  state]
                 + _rglru_weight_specs(lambda c: c) + [_ANY] * len(extra),
        out_specs=[pl.BlockSpec((DEC_SEQ, DEC_BATCH, CG), lambda c: (0, 0, c)), state],
        out_shape=[jax.ShapeDtypeStruct((DEC_SEQ, DEC_BATCH, W_A), BF16),
                   jax.ShapeDtypeStruct((N_EVEN, DEC_BATCH, W_A), F32)],
        input_output_aliases={n_in + i: 1 + i for i in range(len(extra))},
        compiler_params=_cparams("parallel"),
        name="rglru_sample",
    )(xg_tm, xg_tm, buf_tm, h_all, *weights, *extra)


HG_ROWS = 128
DEC_PAD = SUB


def _lb_kernel(raw_ref, o_ref):
    raw = raw_ref[...]
    e = jnp.exp(raw - jnp.max(raw, axis=0, keepdims=True))
    p = e / jnp.sum(e, axis=0, keepdims=True)
    acc = jnp.zeros((1, F_B), F32)
    for r in range(N_EVEN):
        acc = acc + p[r:r + 1, :]
        o_ref[r:r + 1, :] = acc - p[0:1, :]


def _hgrn_lower_bounds(raw):
    return pl.pallas_call(_lb_kernel, out_shape=jax.ShapeDtypeStruct((N_EVEN, F_B), F32),
                          name="hgrn_lower_bounds")(raw)


def _hgrn_chunk(qb, fb, v, lb, st, n_valid, transposed=True):
    c = qb.shape[0]
    rows = lax.broadcasted_iota(jnp.int32, (c, DK_B), 0)
    f = lb + (1.0 - lb) * jax.nn.sigmoid(fb)
    kk = (1.0 - lb) * jax.nn.sigmoid(-fb)
    q = jax.nn.silu(qb)
    cum = _cumsum_rows(jnp.log(f))
    last = cum[n_valid - 1:n_valid, :]
    kdec = kk * jnp.exp(last - cum)
    if n_valid < c:
        kdec = jnp.where(rows < n_valid, kdec, 0.0)
    qdec = (q * jnp.exp(cum)).astype(BF16)
    if transposed:
        o_inter = _dot(qdec, st.astype(BF16), _NT)
        st_new = st * jnp.exp(last) + _dot(v.astype(BF16), kdec.astype(BF16), _TN)
    else:
        o_inter = _dot(qdec, st.astype(BF16))
        decay_col = jnp.broadcast_to(jnp.exp(last), (SUB, DK_B)).T[:, 0:1]
        st_new = st * decay_col + _dot(kdec.astype(BF16), v.astype(BF16), _TN)

    rows8 = lax.broadcasted_iota(jnp.int32, (SUB, DK_B), 0)
    cum2 = cum * LOG2E
    src2 = cum2 - jnp.log2(kk)
    o = o_inter
    for s in range(n_valid):
        t0 = (s // SUB) * SUB
        d = cum2[t0:c, :] - src2[s:s + 1, :]
        head = jnp.where(rows8 >= s - t0, d[0:SUB, :], NEG)
        d = head if t0 + SUB == c else jnp.concatenate([head, d[SUB:, :]], axis=0)
        p = jnp.exp2(d) * q[t0:c, :]
        term = jnp.sum(p, axis=1, keepdims=True) * v[s:s + 1, :]
        o = term + o if t0 == 0 else jnp.concatenate([o[0:t0, :], term + o[t0:c, :]], axis=0)
    return o, st_new


def _hgrn_finish(o, nb_ref, g_ref, y_ref):
    y_ref[...] = (_rms_scale(o) * nb_ref[...] * jax.nn.silu(g_ref[...])).astype(y_ref.dtype)


def _hgrn_prompt_kernel(q_ref, f_ref, i_ref, g_ref, qe_ref, fe_ref, ie_ref, ge_ref, lb_ref, nb_ref,
                        ym_ref, ye_ref, s_ref, st_scr, o_scr):
    j = pl.program_id(1)

    def run_rows(qr, fr, ir, n_rows, chunk):
        def head(h, carry):
            lanes = pl.ds(pl.multiple_of(h * DK_B, DK_B), DK_B)
            lb = lb_ref[:, lanes]
            st = st_scr[h]
            for c0 in range(0, n_rows, chunk):
                rs = slice(c0, c0 + chunk)
                o, st = _hgrn_chunk(qr[rs, lanes], fr[rs, lanes], ir[rs, lanes], lb, st, chunk)
                o_scr[rs, lanes] = o
            st_scr[h] = st
            return carry
        lax.fori_loop(0, H_B, head, 0, unroll=4)

    @pl.when(j == 0)
    def _():
        st_scr[...] = jnp.zeros(st_scr.shape, F32)
        run_rows(qe_ref, fe_ref, ie_ref, N_META, N_META)
        _hgrn_finish(o_scr[0:N_META, :], nb_ref, ge_ref, ye_ref)

    run_rows(q_ref, f_ref, i_ref, HG_ROWS, HGRN_CHUNK)
    _hgrn_finish(o_scr[...], nb_ref, g_ref, ym_ref)

    @pl.when(j == pl.num_programs(1) - 1)
    def _():
        def head(h, carry):
            s_ref[h] = st_scr[h].T
            return carry
        lax.fori_loop(0, H_B, head, 0)


def _hgrn_prompt(proj, y, lb, norm_b, e, s_prev):
    nm = ROW_META0 // N_META
    nj = SEQ // HG_ROWS
    main = lambda col: pl.BlockSpec((HG_ROWS, W_B), lambda b, j: (b * nj + j, col))
    meta = lambda col: pl.BlockSpec((N_META, W_B), lambda b, j: (nm + b, col))
    vec = pl.BlockSpec((1, W_B), lambda b, j: (0, 0))
    n_in = 10
    extra = [y] + ([] if s_prev is None else [s_prev])
    aliases = {n_in: 0}
    if s_prev is not None:
        aliases[n_in + 1] = 2
    return pl.pallas_call(
        _passthrough(_hgrn_prompt_kernel, n_in, len(extra)),
        grid=(BATCH, nj),
        in_specs=[main(2), main(3), main(4), main(5), meta(2), meta(3), meta(4), meta(5), vec, vec]
                 + [_ANY] * len(extra),
        out_specs=[pl.BlockSpec((HG_ROWS, W_B), lambda b, j: (b * nj + j, W_A // W_B)),
                   pl.BlockSpec((N_META, W_B), lambda b, j: (b, 0)),
                   pl.BlockSpec((None, None, H_B, DK_B, DV_B), lambda b, j: (e, b, 0, 0, 0))],
        out_shape=[jax.ShapeDtypeStruct((ROWS, D_MODEL), BF16),
                   jax.ShapeDtypeStruct((ROWS_META, W_B), BF16),
                   jax.ShapeDtypeStruct((N_EVEN, BATCH, H_B, DK_B, DV_B), F32)],
        scratch_shapes=[pltpu.VMEM((H_B, DV_B, DK_B), F32), pltpu.VMEM((HG_ROWS, W_B), F32)],
        input_output_aliases=aliases,
        compiler_params=_cparams("parallel", "arbitrary"),
        name="hgrn_prompt",
    )(*([proj] * 8), lb.reshape(1, F_B), norm_b.reshape(1, W_B), *extra)


def _hgrn_sample_kernel(q_ref, f_ref, i_ref, g_ref, lb_ref, nb_ref, s_in_ref, y_ref, s_out_ref, o_scr):
    for h in range(H_B):
        lanes = slice(h * DK_B, (h + 1) * DK_B)
        q, f, v = (_pad_rows(r[:, lanes], DEC_PAD) for r in (q_ref, f_ref, i_ref))
        o, st = _hgrn_chunk(q, f, v, lb_ref[:, lanes], s_in_ref[h], DEC_SEQ, transposed=False)
        o_scr[:, lanes] = o
        s_out_ref[h] = st
    _hgrn_finish(o_scr[0:DEC_SEQ, :], nb_ref, g_ref, y_ref)


def _hgrn_sample(proj_s, lb, norm_b, s_all, e, s_prev):
    part = lambda col: pl.BlockSpec((None, DEC_SEQ, W_B), lambda b: (b, 0, col))
    vec = pl.BlockSpec((1, W_B), lambda b: (0, 0))
    state = pl.BlockSpec((None, None, H_B, DK_B, DV_B), lambda b: (e, b, 0, 0, 0))
    n_in = 7
    extra = [] if s_prev is None else [s_prev]
    return pl.pallas_call(
        _passthrough(_hgrn_sample_kernel, n_in, len(extra)),
        grid=(DEC_BATCH,),
        in_specs=[part(2), part(3), part(4), part(5), vec, vec, state] + [_ANY] * len(extra),
        out_specs=[pl.BlockSpec((None, DEC_SEQ, W_B), lambda b: (b, 0, 0)), state],
        out_shape=[jax.ShapeDtypeStruct((DEC_BATCH, DEC_SEQ, W_B), F32),
                   jax.ShapeDtypeStruct((N_EVEN, DEC_BATCH, H_B, DK_B, DV_B), F32)],
        scratch_shapes=[pltpu.VMEM((DEC_PAD, W_B), F32)],
        input_output_aliases={n_in + i: 1 + i for i in range(len(extra))},
        compiler_params=_cparams("parallel"),
        name="hgrn_sample",
    )(*([proj_s] * 4), lb.reshape(1, F_B), norm_b.reshape(1, W_B), s_all, *extra)


MLSTM_GROUP_PROMPT = 2


def _mlstm_gates(ig, fg, m_prev, n_valid):
    c = ig.shape[0]
    ti = lax.broadcasted_iota(jnp.int32, (c, c), 0)
    si = lax.broadcasted_iota(jnp.int32, (c, c), 1)
    tri = si <= ti
    to_row = lambda col: jnp.sum(jnp.where(ti == si, col, 0.0), axis=0, keepdims=True)
    cum = jnp.sum(jnp.where(tri, to_row(jax.nn.log_sigmoid(fg)), 0.0), axis=1, keepdims=True)
    logw = jnp.where(tri, cum - to_row(cum) + to_row(ig), NEG)
    log_inter = cum + m_prev
    m_t = jnp.maximum(log_inter, jnp.max(logw, axis=1, keepdims=True))
    w = jnp.exp(logw - m_t)
    g = jnp.exp(log_inter - m_t)
    lv = n_valid - 1
    m_new = m_t[lv:lv + 1, :]
    cum_last = cum[lv:lv + 1, :]
    ws = jnp.exp(cum_last - cum + ig - m_new)
    if n_valid < c:
        ws = jnp.where(lax.broadcasted_iota(jnp.int32, (c, 1), 0) < n_valid, ws, 0.0)
    decay = jnp.exp(cum_last + m_prev - m_new)
    return w, g, m_t, ws, decay, m_new


def _mlstm_heads(q_ref, k_ref, v_ref, o_ref, g_ref, bias_ref, nc_ref, y_ref, get_state, put_state, n_valid, group):
    n_rows = q_ref.shape[0]
    ld = lambda ref, cols: _pad_rows(ref[:, cols], -(-n_rows // SUB) * SUB)
    for h0 in range(0, H_C, group):
        heads = range(h0, h0 + group)
        kcols = {h: slice(h * DK_C, (h + 1) * DK_C) for h in heads}
        vcols = {h: slice(h * DV_C, (h + 1) * DV_C) for h in heads}
        state = {h: get_state(h) for h in heads}
        gates = {}
        for h in heads:
            ig = ld(g_ref, slice(h, h + 1)) + bias_ref[0:1, h:h + 1]
            fg = ld(g_ref, slice(H_C + h, H_C + h + 1)) + bias_ref[0:1, H_C + h:H_C + h + 1]
            gates[h] = _mlstm_gates(ig, fg, state[h][2], n_valid)
        qs = {h: ld(q_ref, kcols[h]) * (DK_C ** -0.5) for h in heads}
        qb = {h: qs[h].astype(BF16) for h in heads}
        kf = {h: ld(k_ref, kcols[h]) for h in heads}
        vb = {h: ld(v_ref, vcols[h]).astype(BF16) for h in heads}
        qk = {h: _dot(qb[h], kf[h].astype(BF16), _NT) for h in heads}
        inter = {h: _dot(qb[h], state[h][0].astype(BF16)) for h in heads}
        p = {h: qk[h] * gates[h][0] for h in heads}
        intra = {h: _dot(p[h].astype(BF16), vb[h]) for h in heads}
        for h in heads:
            w, g, m_t, ws, decay, m_new = gates[h]
            num = g * inter[h] + intra[h]
            den = g * jnp.sum(qs[h] * state[h][1], axis=1, keepdims=True) + jnp.sum(p[h], axis=1, keepdims=True)
            hh = num / jnp.maximum(jnp.abs(den), jnp.exp(-m_t))
            hn = (_rms_scale(hh) * nc_ref[:, vcols[h]])[0:n_rows, :]
            y_ref[:, vcols[h]] = (jax.nn.sigmoid(o_ref[:, vcols[h]]) * hn).astype(y_ref.dtype)
        kw = {h: kf[h] * gates[h][3] for h in heads}
        upd = {h: _dot(kw[h].astype(BF16), vb[h], _TN) for h in heads}
        for h in heads:
            C, n_row, _ = state[h]
            decay, m_new = gates[h][4], gates[h][5]
            put_state(h, decay * C + upd[h], decay * n_row + jnp.sum(kw[h], axis=0, keepdims=True), m_new)


def _mlstm_prompt_kernel(q_ref, k_ref, v_ref, o_ref, g_ref, qe_ref, ke_ref, ve_ref, oe_ref, ge_ref, bias_ref, nc_ref,
                         ym_ref, ye_ref, c_ref, n_ref, m_ref):
    def get_state(h):
        return c_ref[h], n_ref[h:h + 1, :], m_ref[0:1, h:h + 1]

    def put_state(h, C, n_row, m_new):
        c_ref[h] = C
        n_ref[h:h + 1, :] = n_row
        m_ref[0:1, h:h + 1] = m_new

    @pl.when(pl.program_id(1) == 0)
    def _():
        c_ref[...] = jnp.zeros(c_ref.shape, F32)
        n_ref[...] = jnp.zeros(n_ref.shape, F32)
        m_ref[...] = jnp.zeros(m_ref.shape, F32)
        _mlstm_heads(qe_ref, ke_ref, ve_ref, oe_ref, ge_ref, bias_ref, nc_ref, ye_ref, get_state, put_state, N_META,
                     MLSTM_GROUP_PROMPT)

    _mlstm_heads(q_ref, k_ref, v_ref, o_ref, g_ref, bias_ref, nc_ref, ym_ref, get_state, put_state, MLSTM_CHUNK,
                 MLSTM_GROUP_PROMPT)


def _mlstm_state_specs(d, index):
    return [pl.BlockSpec((None, None, H_C, DK_C, DV_C), lambda *g: (d, index(*g), 0, 0, 0)),
            pl.BlockSpec((None, None, H_C, DK_C), lambda *g: (d, index(*g), 0, 0)),
            pl.BlockSpec((None, None, 1, H_C), lambda *g: (d, index(*g), 0, 0))]


def _mlstm_state_shapes(batch):
    return [jax.ShapeDtypeStruct((N_ODD, batch, H_C, DK_C, DV_C), F32),
            jax.ShapeDtypeStruct((N_ODD, batch, H_C, DK_C), F32),
            jax.ShapeDtypeStruct((N_ODD, batch, 1, H_C), F32)]


def _mlstm_prompt(proj, gates, bias, norm_c, d, prev):
    nm = ROW_META0 // N_META
    nj = SEQ // MLSTM_CHUNK
    main = lambda width, col: pl.BlockSpec((MLSTM_CHUNK, width), lambda b, j: (b * nj + j, col))
    meta = lambda width, col: pl.BlockSpec((N_META, width), lambda b, j: (nm + b, col))
    n_in = 12
    extra = [] if prev is None else list(prev)
    return pl.pallas_call(
        _passthrough(_mlstm_prompt_kernel, n_in, len(extra)),
        grid=(BATCH, nj),
        in_specs=[main(W_CK, 0), main(W_CK, 1), main(W_CV, 1), main(W_CV, 2), main(LANES, 0),
                  meta(W_CK, 0), meta(W_CK, 1), meta(W_CV, 1), meta(W_CV, 2), meta(LANES, 0),
                  pl.BlockSpec((1, LANES), lambda b, j: (0, 0)), pl.BlockSpec((1, W_CV), lambda b, j: (0, 0))]
                 + [_ANY] * len(extra),
        out_specs=[pl.BlockSpec((MLSTM_CHUNK, W_CV), lambda b, j: (b * nj + j, 0)),
                   pl.BlockSpec((N_META, W_CV), lambda b, j: (b, 0))]
                  + _mlstm_state_specs(d, lambda b, j: b),
        out_shape=[jax.ShapeDtypeStruct((ROWS, W_CV), BF16),
                   jax.ShapeDtypeStruct((ROWS_META, W_CV), BF16)] + _mlstm_state_shapes(BATCH),
        input_output_aliases={n_in + i: 2 + i for i in range(len(extra))},
        compiler_params=_cparams("parallel", "arbitrary"),
        name="mlstm_prompt",
    )(proj, proj, proj, proj, gates, proj, proj, proj, proj, gates, bias, norm_c.reshape(1, W_CV), *extra)


def _mlstm_sample_kernel(q_ref, k_ref, v_ref, o_ref, g_ref, bias_ref, nc_ref, c0_ref, n0_ref, m0_ref,
                         y_ref, c_ref, n_ref, m_ref):
    def get_state(h):
        return c0_ref[h], n0_ref[h:h + 1, :], m0_ref[0:1, h:h + 1]

    def put_state(h, C, n_row, m_new):
        c_ref[h] = C
        n_ref[h:h + 1, :] = n_row
        m_ref[0:1, h:h + 1] = m_new

    _mlstm_heads(q_ref, k_ref, v_ref, o_ref, g_ref, bias_ref, nc_ref, y_ref, get_state, put_state, DEC_SEQ, H_C)


def _mlstm_sample(proj_s, gates_s, bias, norm_c, c_all, n_all, m_all, d, prev):
    part = lambda width, col: pl.BlockSpec((None, DEC_SEQ, width), lambda b: (b, 0, col))
    states = _mlstm_state_specs(d, lambda b: b)
    n_in = 10
    extra = [] if prev is None else list(prev)
    return pl.pallas_call(
        _passthrough(_mlstm_sample_kernel, n_in, len(extra)),
        grid=(DEC_BATCH,),
        in_specs=[part(W_CK, 0), part(W_CK, 1), part(W_CV, 1), part(W_CV, 2), part(LANES, 0),
                  pl.BlockSpec((1, LANES), lambda b: (0, 0)), pl.BlockSpec((1, W_CV), lambda b: (0, 0))]
                 + states + [_ANY] * len(extra),
        out_specs=[pl.BlockSpec((None, DEC_SEQ, W_CV), lambda b: (b, 0, 0))] + states,
        out_shape=[jax.ShapeDtypeStruct((DEC_BATCH, DEC_SEQ, W_CV), F32)] + _mlstm_state_shapes(DEC_BATCH),
        input_output_aliases={n_in + i: 1 + i for i in range(len(extra))},
        compiler_params=_cparams("parallel"),
        name="mlstm_sample",
    )(proj_s, proj_s, proj_s, proj_s, gates_s, bias, norm_c.reshape(1, W_CV), c_all, n_all, m_all, *extra)


def _sample_rows(a):
    return a[ROW_S0:ROW_META0].reshape(DEC_BATCH, DEC_SEQ, a.shape[-1])


def _fill_tail_rows(y, sample, meta):
    cat = lambda parts: parts[0] if len(parts) == 1 else jnp.concatenate(parts, axis=1)
    tail = jnp.concatenate([cat(sample), cat(meta), jnp.zeros((ROWS_PAD, y.shape[1]), y.dtype)], axis=0)
    return lax.dynamic_update_slice(y, tail, (ROWS_P, 0))


def _mix_even(proj, e, conv_all, h_all, s_all, rg_weights, lb, norm_b, prev):
    p_h0, p_s0, s_h0, s_s0 = prev if prev is not None else (None,) * 4
    y, ya_e, p_h = _rglru_prompt(proj, rg_weights, e, p_h0)
    y, yb_e, p_s = _hgrn_prompt(proj, y, lb, norm_b, e, p_s0)
    ps = _sample_rows(proj)
    xg_tm = jnp.swapaxes(ps[:, :, :2 * W_A], 0, 1)
    ya_s, s_h = _rglru_sample(xg_tm, jnp.swapaxes(conv_all[e], 0, 1), h_all, rg_weights, e, s_h0)
    ya_s = jnp.swapaxes(ya_s, 0, 1).reshape(ROWS_S, W_A)
    yb_s, s_s = _hgrn_sample(ps, lb, norm_b, s_all, e, s_s0)
    yb_s = yb_s.reshape(ROWS_S, W_B).astype(BF16)
    y = _fill_tail_rows(y, [ya_s, yb_s], [ya_e, yb_e])
    conv_p = jnp.stack([proj[(b + 1) * SEQ - (CONV_W - 1):(b + 1) * SEQ, :W_A] for b in range(BATCH)])
    conv_s = ps[:, DEC_SEQ - (CONV_W - 1):, :W_A]
    return y, conv_p, conv_s, (p_h, p_s, s_h, s_s)


def _mix_odd(proj, gates, bias, norm_c, d, c_all, n_all, m_all, prev):
    p_prev, s_prev = (prev[:3], prev[3:]) if prev is not None else (None, None)
    y, y_e, *p_state = _mlstm_prompt(proj, gates, bias, norm_c, d, p_prev)
    y_s, *s_state = _mlstm_sample(_sample_rows(proj), _sample_rows(gates), bias, norm_c, c_all, n_all, m_all, d,
                                  s_prev)
    y_s = y_s.reshape(ROWS_S, W_CV).astype(BF16)
    y = _fill_tail_rows(y, [y_s], [y_e])
    return y, (*p_state, *s_state)


def kernel(x_prompt, x_sample, state_rglru_conv, state_rglru_h, state_hgrn_S, state_mlstm_C, state_mlstm_n, state_mlstm_m, meta_tokens, ffn1_norm, ffn1_w_gate, ffn1_w_up, ffn1_w_down, mix_norm, ffn2_norm, ffn2_w_gate, ffn2_w_up, ffn2_w_down, even_w_in, rglru_conv_w, rglru_conv_b, rglru_w_r, rglru_b_r, rglru_w_i, rglru_b_i, rglru_lambda, hgrn_lb_raw, hgrn_norm, even_w_out, odd_w_in, mlstm_b_gates, mlstm_norm, odd_w_out, final_norm):
    meta = jnp.broadcast_to(meta_tokens[None], (BATCH, N_META, D_MODEL)).reshape(ROWS_META, D_MODEL)
    x = jnp.concatenate([x_prompt.reshape(ROWS_P, D_MODEL), x_sample.reshape(ROWS_S, D_MODEL), meta,
                         jnp.zeros((ROWS_PAD, D_MODEL), F32)], axis=0)
    lb_all = _hgrn_lower_bounds(hgrn_lb_raw)
    m_all = state_mlstm_m.reshape(N_ODD, DEC_BATCH, 1, H_C)
    odd_w_in_t = jnp.swapaxes(odd_w_in, 1, 2)
    p_conv, s_conv = [], []
    even_states = odd_states = None

    for l in range(DEPTH):
        x = _ffn_half(x, ffn1_norm, ffn1_w_gate, ffn1_w_up, ffn1_w_down, l)
        u = _rmsnorm(x, mix_norm[l], BF16)
        if l % 2 == 0:
            e = l // 2
            proj = _matmul(u, even_w_in, e, IN_EVEN)
            rg_weights = _rglru_weights(rglru_conv_w[e], rglru_conv_b[e], rglru_w_r[e], rglru_b_r[e], rglru_w_i[e],
                                        rglru_b_i[e], rglru_lambda[e])
            y, conv_p, conv_s, even_states = _mix_even(proj, e, state_rglru_conv, state_rglru_h, state_hgrn_S,
                                                       rg_weights, lb_all[e], hgrn_norm[e], even_states)
            p_conv.append(conv_p)
            s_conv.append(conv_s)
            x = _matmul_residual(y, even_w_out, e, x, 1.0)
        else:
            d = l // 2
            proj = _matmul_t(u, odd_w_in_t, d, IN_ODD_MAIN)
            gates = _matmul_tail(u, odd_w_in_t, d, IN_ODD_MAIN)
            bias = jnp.pad(mlstm_b_gates[d].reshape(1, 2 * H_C), ((0, 0), (0, LANES - 2 * H_C)))
            y, odd_states = _mix_odd(proj, gates, bias, mlstm_norm[d], d, state_mlstm_C, state_mlstm_n, m_all,
                                     odd_states)
            x = _matmul_residual(y, odd_w_out, d, x, 1.0)
        x = _ffn_half(x, ffn2_norm, ffn2_w_gate, ffn2_w_up, ffn2_w_down, l)

    y_prompt = _rmsnorm(x, final_norm, F32, TR_OUT, 0, ROWS_P).reshape(BATCH, SEQ, D_MODEL)
    y_sample = _rmsnorm(x, final_norm, F32, TR_OUT, ROW_S0, ROWS_S).reshape(DEC_BATCH, DEC_SEQ, D_MODEL)
    p_h, p_s, s_h, s_s = even_states
    p_c, p_n, p_m, s_c, s_n, s_m = odd_states
    return (y_prompt, y_sample,
            jnp.stack(p_conv), p_h.reshape(N_EVEN, BATCH, W_A), p_s, p_c, p_n, p_m.reshape(N_ODD, BATCH, H_C),
            jnp.stack(s_conv), s_h, s_s, s_c, s_n, s_m.reshape(N_ODD, DEC_BATCH, H_C))
```
